```python
import math
import jax, jax.numpy as jnp
from jax import lax
import numpy as np

D_MODEL = 2048
BATCH = 16
SEQ = 2048
DEPTH = 4

GRID_W = 64
CTX_LEN = 256
N_MIXERS = 2
N_HEADS = 8
HEAD_DIM = D_MODEL // N_HEADS // 2
V_DIM = 2 * HEAD_DIM
ROPE_BASE = 10000.0
Q_BLOCK = 128
CONV_WIDTH = 31
CONV_PAD = (CONV_WIDTH - 1) // 2
D_FF = 256 * ((8 * D_MODEL // 3 + 255) // 256)
N_EXPERTS = 8
TOP_K = 2
N_ATTN = (DEPTH + 1) // 2
N_CONV = DEPTH // 2
N_DENSE = (DEPTH + 1) // 2
N_MOE = DEPTH // 2
N_MOD = 6
NORM_EPS = 1e-6

kernel_name = "hybrid_diffattn_conformer_moe_dit"


def rms_norm(x, g):
    xf = x.astype(jnp.float32)
    xf = xf * lax.rsqrt(jnp.mean(xf * xf, axis=-1, keepdims=True) + NORM_EPS)
    return (xf * g.astype(jnp.float32)).astype(x.dtype)


def layer_norm(x, g, b):
    xf = x.astype(jnp.float32)
    xc = xf - jnp.mean(xf, axis=-1, keepdims=True)
    var = jnp.mean(xc * xc, axis=-1, keepdims=True)
    return (xc * lax.rsqrt(var + NORM_EPS) * g.astype(jnp.float32) + b.astype(jnp.float32)).astype(x.dtype)


def ada_mods(cond, w, b):
    m = jax.nn.silu(cond) @ w + b
    return jnp.split(m[:, None, :], N_MOD, axis=-1)


def modulate(x, g, shift, scale):
    return rms_norm(x, g) * (1 + scale) + shift


def axial_rope(seq_len):
    rows = seq_len // GRID_W
    row = jnp.repeat(jnp.arange(rows, dtype=jnp.float32), GRID_W)
    col = jnp.tile(jnp.arange(GRID_W, dtype=jnp.float32), rows)
    n_freq = HEAD_DIM // 4
    inv_freq = ROPE_BASE ** (-jnp.arange(n_freq, dtype=jnp.float32) / n_freq)
    ang = jnp.concatenate([row[:, None] * inv_freq, col[:, None] * inv_freq], axis=-1)
    ang = jnp.concatenate([ang, ang], axis=-1)
    return jnp.cos(ang), jnp.sin(ang)


def apply_rope(x, cos, sin):
    xf = x.astype(jnp.float32)
    x1, x2 = jnp.split(xf, 2, axis=-1)
    rot = jnp.concatenate([-x2, x1], axis=-1)
    c = cos[None, :, None, None, :]
    s = sin[None, :, None, None, :]
    return (xf * c + rot * s).astype(x.dtype)


def diff_attend(q, k, v, lam):
    s = jnp.einsum('bqhcd,bkhcd->bhcqk', q, k, preferred_element_type=jnp.float32) * (HEAD_DIM ** -0.5)
    p = jax.nn.softmax(s, axis=-1)
    a = (p[:, :, 0] - lam * p[:, :, 1]).astype(v.dtype)
    return jnp.einsum('bhqk,bkhe->bqhe', a, v)


def diff_attention(h_lat, h_ctx, w_qkv, lam_p, subln_g, w_o, lambda_init, cos, sin, ctx_out):
    B, S, _ = h_lat.shape
    L = h_ctx.shape[1]
    q, k, v = jnp.split(h_lat @ w_qkv, 3, axis=-1)
    q = apply_rope(q.reshape(B, S, N_HEADS, 2, HEAD_DIM), cos, sin)
    k = apply_rope(k.reshape(B, S, N_HEADS, 2, HEAD_DIM), cos, sin)
    v = v.reshape(B, S, N_HEADS, V_DIM)
    k_c, v_c = jnp.split(h_ctx @ w_qkv[:, D_MODEL:], 2, axis=-1)
    k_c = k_c.reshape(B, L, N_HEADS, 2, HEAD_DIM)
    v_c = v_c.reshape(B, L, N_HEADS, V_DIM)
    k_all = jnp.concatenate([k_c, k], axis=1)
    v_all = jnp.concatenate([v_c, v], axis=1)
    lp = lam_p.astype(jnp.float32)
    lam = jnp.exp(jnp.sum(lp[0] * lp[1])) - jnp.exp(jnp.sum(lp[2] * lp[3])) + lambda_init

    def head_out(o):
        o = rms_norm(o, subln_g) * (1.0 - lambda_init)
        return o.reshape(B, o.shape[1], D_MODEL) @ w_o

    nb = S // Q_BLOCK
    qb = q.reshape(B, nb, Q_BLOCK, N_HEADS, 2, HEAD_DIM).swapaxes(0, 1)
    o = lax.map(lambda q_blk: diff_attend(q_blk, k_all, v_all, lam), qb)
    y_lat = head_out(o.swapaxes(0, 1).reshape(B, S, N_HEADS, V_DIM))
    y_ctx = None
    if ctx_out:
        q_c = (h_ctx @ w_qkv[:, :D_MODEL]).reshape(B, L, N_HEADS, 2, HEAD_DIM)
        y_ctx = head_out(diff_attend(q_c, k_c, v_c, lam))
    return y_lat, y_ctx


def conformer_conv(h, w_in, b_in, w_dw, b_dw, ln_g, ln_b, w_out, b_out):
    u = h @ w_in + b_in
    u = u[..., :D_MODEL] * jax.nn.sigmoid(u[..., D_MODEL:])
    u = lax.conv_general_dilated(u, w_dw[:, None, :], window_strides=(1,),
                                 padding=[(CONV_PAD, CONV_PAD)],
                                 dimension_numbers=('NWC', 'WIO', 'NWC'),
                                 feature_group_count=D_MODEL) + b_dw
    u = jax.nn.silu(layer_norm(u, ln_g, ln_b))
    return u @ w_out + b_out


def swiglu(h, w_gu, w_down):
    g, u = jnp.split(h @ w_gu, 2, axis=-1)
    return (jax.nn.silu(g) * u) @ w_down


def moe_swiglu(h, w_router, w_gu, w_down):
    logits = (h @ w_router).astype(jnp.float32)
    top_val, top_idx = lax.top_k(logits, TOP_K)
    top_w = jax.nn.softmax(top_val, axis=-1)
    gates = jnp.einsum('blk,blke->ble', top_w,
                       jax.nn.one_hot(top_idx, N_EXPERTS, dtype=jnp.float32)).astype(h.dtype)
    out = jnp.zeros_like(h)
    for e in range(N_EXPERTS):
        out = out + gates[..., e:e + 1] * swiglu(h, w_gu[e], w_down[e])
    return out


def setup_inputs(seed: int = 0) -> dict:
    key = jax.random.key(seed)
    ks = jax.random.split(key, 32)
    D = D_MODEL

    def nrm(k, shape, scale):
        return jax.random.normal(k, shape, jnp.float32) * scale

    return {
        "x": nrm(ks[0], (BATCH, SEQ, D), 1.0),
        "c": nrm(ks[1], (BATCH, D), 1.0),
        "ctx": nrm(ks[2], (BATCH, CTX_LEN, D), 1.0),
        "c_ctx": nrm(ks[3], (D,), 1.0),
        "ada_w": nrm(ks[4], (DEPTH, D, N_MOD * D), 0.5 * D ** -0.5),
        "ada_b": nrm(ks[5], (DEPTH, N_MOD * D), 0.02),
        "norm_mix_g": 1.0 + nrm(ks[6], (DEPTH, D), 0.05),
        "norm_ffn_g": 1.0 + nrm(ks[7], (DEPTH, D), 0.05),
        "attn_w_qkv": nrm(ks[8], (N_ATTN, D, 3 * D), D ** -0.5),
        "attn_lambda": nrm(ks[9], (N_ATTN, 4, HEAD_DIM), 0.1),
        "attn_subln_g": 1.0 + nrm(ks[10], (N_ATTN, V_DIM), 0.05),
        "attn_w_o": nrm(ks[11], (N_ATTN, D, D), D ** -0.5),
        "conv_w_in": nrm(ks[12], (N_CONV, D, 2 * D), D ** -0.5),
        "conv_b_in": nrm(ks[13], (N_CONV, 2 * D), 0.02),
        "conv_w_dw": nrm(ks[14], (N_CONV, CONV_WIDTH, D), CONV_WIDTH ** -0.5),
        "conv_b_dw": nrm(ks[15], (N_CONV, D), 0.02),
        "conv_ln_g": 1.0 + nrm(ks[16], (N_CONV, D), 0.05),
        "conv_ln_b": nrm(ks[17], (N_CONV, D), 0.02),
        "conv_w_out": nrm(ks[18], (N_CONV, D, D), D ** -0.5),
        "conv_b_out": nrm(ks[19], (N_CONV, D), 0.02),
        "ffn_w_gu": nrm(ks[20], (N_DENSE, D, 2 * D_FF), D ** -0.5),
        "ffn_w_down": nrm(ks[21], (N_DENSE, D_FF, D), D_FF ** -0.5),
        "moe_router": nrm(ks[22], (N_MOE, D, N_EXPERTS), D ** -0.5),
        "moe_w_gu": nrm(ks[23], (N_MOE, N_EXPERTS, D, 2 * D_FF), D ** -0.5),
        "moe_w_down": nrm(ks[24], (N_MOE, N_EXPERTS, D_FF, D), D_FF ** -0.5),
        "final_g": 1.0 + nrm(ks[25], (D,), 0.05),
    }


def reference(x, c, ctx, c_ctx, ada_w, ada_b, norm_mix_g, norm_ffn_g,
              attn_w_qkv, attn_lambda, attn_subln_g, attn_w_o,
              conv_w_in, conv_b_in, conv_w_dw, conv_b_dw, conv_ln_g, conv_ln_b, conv_w_out, conv_b_out,
              ffn_w_gu, ffn_w_down, moe_router, moe_w_gu, moe_w_down, final_g):
    seq_len = x.shape[1]
    cos, sin = axial_rope(seq_len)
    attn_layers = [i for i in range(DEPTH) if i % N_MIXERS == 0]
    last_ctx_reader = attn_layers[-1]
    h_lat, h_ctx = x, ctx
    for i in range(DEPTH):
        mix_idx = i // N_MIXERS
        ffn_idx = i // 2
        ctx_used = i <= last_ctx_reader
        ctx_live = i < last_ctx_reader
        sm, cm, gm, sf, cf, gf = ada_mods(c, ada_w[i], ada_b[i])
        hl = modulate(h_lat, norm_mix_g[i], sm, cm)
        if ctx_used:
            smc, cmc, gmc, sfc, cfc, gfc = ada_mods(c_ctx[None, :], ada_w[i], ada_b[i])
            hc = modulate(h_ctx, norm_mix_g[i], smc, cmc)
        if i % N_MIXERS == 0:
            lambda_init = 0.8 - 0.6 * math.exp(-0.3 * i)
            y_lat, y_ctx = diff_attention(hl, hc, attn_w_qkv[mix_idx], attn_lambda[mix_idx],
                                          attn_subln_g[mix_idx], attn_w_o[mix_idx], lambda_init,
                                          cos, sin, ctx_live)
        else:
            conv_p = (conv_w_in[mix_idx], conv_b_in[mix_idx], conv_w_dw[mix_idx], conv_b_dw[mix_idx],
                      conv_ln_g[mix_idx], conv_ln_b[mix_idx], conv_w_out[mix_idx], conv_b_out[mix_idx])
            y_lat = conformer_conv(hl, *conv_p)
            y_ctx = conformer_conv(hc, *conv_p) if ctx_live else None
        h_lat = h_lat + gm * y_lat
        if ctx_live:
            h_ctx = h_ctx + gmc * y_ctx
        if i % 2 == 0:
            ffn = lambda h: swiglu(h, ffn_w_gu[ffn_idx], ffn_w_down[ffn_idx])
        else:
            ffn = lambda h: moe_swiglu(h, moe_router[ffn_idx], moe_w_gu[ffn_idx], moe_w_down[ffn_idx])
        h_lat = h_lat + gf * ffn(modulate(h_lat, norm_ffn_g[i], sf, cf))
        if ctx_live:
            h_ctx = h_ctx + gfc * ffn(modulate(h_ctx, norm_ffn_g[i], sfc, cfc))
    return rms_norm(h_lat, final_g)
```

```python
import functools
import math

import jax
import jax.numpy as jnp
from jax import lax
from jax.experimental import pallas as pl
from jax.experimental.pallas import tpu as pltpu

F32 = jnp.float32
BF16 = jnp.bfloat16

D = 2048
B = 16
S = 2048
L = 256
DEPTH = 4
GRID_W = 64
H = 8
HD = 128
VD = 2 * HD
ROPE_BASE = 10000.0
CW = 31
CPAD = (CW - 1) // 2
DFF = 5632
E = 8
NMOD = 6
EPS = 1e-6

T_LAT = B * S
T_CTX = B * L
T_ALL = T_LAT + T_CTX
COND_ROWS = 24
CTX_COND_ROW = B
LAST_CTX_READER = 2

V7X_VMEM_LIMIT_BYTES = 56 * 1024 * 1024
LANES = 128
HALO = 16

TM = 1024
BN = 512
TQ = 512
TS = 256
TME = 512
BN_UP = 512
BN_DOWN = 1024
PERM_ROWS = 256
N_TILES_E = (2 * T_ALL + E * (TME - 1) + TME - 1) // TME
N_TILES_E_LAT = (2 * T_LAT + E * (TME - 1) + TME - 1) // TME


def _params(*sem):
    return pltpu.CompilerParams(dimension_semantics=sem, vmem_limit_bytes=V7X_VMEM_LIMIT_BYTES)


def _sigmoid(x):
    return 1.0 / (1.0 + jnp.exp(-x))


def _modulate(x, g, shift, scale):
    xn = x * lax.rsqrt(jnp.mean(x * x, axis=-1, keepdims=True) + EPS)
    return (xn * g) * (1.0 + scale) + shift


def _cond_row(i, tm):
    return jnp.where(i * tm < T_LAT, (i * tm) // S, CTX_COND_ROW)


def _mod_spec(layer, which, tm, bn=D, col=False):
    def imap(i, *rest):
        j = rest[0] if col else 0
        return ((layer * COND_ROWS + _cond_row(i, tm)) * NMOD + which, 0, j)
    return pl.BlockSpec((1, 1, bn), imap)


def _layer_vec_spec(idx, bn=D, col=False):
    def imap(i, *rest):
        return (idx, 0, rest[0] if col else 0)
    return pl.BlockSpec((1, 1, bn), imap)


def _ada_body(c_ref, w_ref, b_ref, o_ref):
    x = c_ref[...]
    s = (x * _sigmoid(x)).astype(BF16)
    o_ref[0] = jnp.dot(s, w_ref[0].astype(BF16), preferred_element_type=F32) + b_ref[0]


def _ada_mods(cond, ada_w, ada_b):
    bn = 1024
    n = NMOD * D
    out = pl.pallas_call(
        _ada_body,
        grid=(DEPTH, n // bn),
        in_specs=[
            pl.BlockSpec((COND_ROWS, D), lambda l, j: (0, 0)),
            pl.BlockSpec((1, D, bn), lambda l, j: (l, 0, j)),
            pl.BlockSpec((1, 1, bn), lambda l, j: (l, 0, j)),
        ],
        out_specs=pl.BlockSpec((1, COND_ROWS, bn), lambda l, j: (l, 0, j)),
        out_shape=jax.ShapeDtypeStruct((DEPTH, COND_ROWS, n), F32),
        compiler_params=_params("arbitrary", "arbitrary"),
        name="ada_mods",
    )(cond, ada_w, ada_b.reshape(DEPTH, 1, n))
    return out.reshape(DEPTH * COND_ROWS * NMOD, 1, D)


def _qkv_body(x_ref, g_ref, sh_ref, sc_ref, w_ref, cos_ref, sin_ref, o_ref, hm_ref, *, n_rope):
    j = pl.program_id(1)

    @pl.when(j == 0)
    def _():
        hm_ref[...] = _modulate(x_ref[...], g_ref[0], sh_ref[0], sc_ref[0]).astype(BF16)

    acc = jnp.dot(hm_ref[...], w_ref[...], preferred_element_type=F32)

    @pl.when(j < n_rope)
    def _():
        cos = cos_ref[0]
        sin = sin_ref[0]
        for c in range(BN // HD):
            xc = acc[:, c * HD:(c + 1) * HD]
            o_ref[:, c * HD:(c + 1) * HD] = (xc * cos + pltpu.roll(xc, HD // 2, 1) * sin).astype(BF16)

    @pl.when(j >= n_rope)
    def _():
        o_ref[...] = acc.astype(BF16)


def _qkv_proj(h, mods, g, layer, w_qkv, cos_t, sin_t):
    nt = T_ALL // TM
    nq = D // BN
    tiles_per_seq = S // TM

    def rope_map(i, j):
        return (jnp.where(j < nq, 0, 1),
                jnp.where(i * TM < T_LAT, i % tiles_per_seq, tiles_per_seq), 0)

    return pl.pallas_call(
        functools.partial(_qkv_body, n_rope=2 * nq),
        grid=(nt, 3 * D // BN),
        in_specs=[
            pl.BlockSpec((TM, D), lambda i, j: (i, 0)),
            _layer_vec_spec(layer),
            _mod_spec(layer, 0, TM),
            _mod_spec(layer, 1, TM),
            pl.BlockSpec((D, BN), lambda i, j: (0, j)),
            pl.BlockSpec((1, TM, HD), rope_map),
            pl.BlockSpec((1, TM, HD), rope_map),
        ],
        out_specs=pl.BlockSpec((TM, BN), lambda i, j: (i, j)),
        out_shape=jax.ShapeDtypeStruct((T_ALL, 3 * D), BF16),
        scratch_shapes=[pltpu.VMEM((TM, D), BF16)],
        compiler_params=_params("arbitrary", "arbitrary"),
        name="qkv_proj",
    )(h, g, mods, mods, w_qkv, cos_t, sin_t)


def _attn_body(lam_ref, g_ref, q_ref, kc_ref, vc_ref, *rest, lambda_init, with_lat):
    if with_lat:
        k_ref, v_ref, o_ref = rest
    else:
        o_ref = rest[-1]
    lp = lam_ref[0]
    lam = (jnp.exp(jnp.sum(lp[0:1] * lp[1:2], axis=-1, keepdims=True))
           - jnp.exp(jnp.sum(lp[2:3] * lp[3:4], axis=-1, keepdims=True)) + lambda_init)
    q = q_ref[...]
    nt_dims = (((1,), (1,)), ((), ()))
    parts = []
    for c in range(2):
        sl = slice(c * HD, (c + 1) * HD)
        qc = q[:, sl]
        s_c = lax.dot_general(qc, kc_ref[:, sl], nt_dims, preferred_element_type=F32)
        m = jnp.max(s_c, axis=-1, keepdims=True)
        if with_lat:
            s_l = lax.dot_general(qc, k_ref[:, sl], nt_dims, preferred_element_type=F32)
            m = jnp.maximum(m, jnp.max(s_l, axis=-1, keepdims=True))
            p_l = jnp.exp(s_l - m)
        p_c = jnp.exp(s_c - m)
        den = jnp.sum(p_c, axis=-1, keepdims=True)
        if with_lat:
            den = den + jnp.sum(p_l, axis=-1, keepdims=True)
        parts.append((p_c, p_l if with_lat else None, den))
    (p1c, p1l, den1), (p2c, p2l, den2) = parts
    r = lam * den1 / den2
    o = jnp.dot((p1c - r * p2c).astype(BF16), vc_ref[...], preferred_element_type=F32)
    if with_lat:
        o = o + jnp.dot((p1l - r * p2l).astype(BF16), v_ref[...], preferred_element_type=F32)
    o = o / den1
    o = o * lax.rsqrt(jnp.mean(o * o, axis=-1, keepdims=True) + EPS) * g_ref[0] * (1.0 - lambda_init)
    o_ref[...] = o.astype(BF16)


def _attention(qkv, lam_p, subln_g, mix_idx, lambda_init, ctx_out):
    t_out = T_ALL if ctx_out else T_LAT
    nqt = S // TQ
    kcol, vcol = D // VD, 2 * D // VD
    ctx_blk = T_LAT // L
    common = [
        pl.BlockSpec((1, 4, HD), lambda b, h, t: (mix_idx, 0, 0)),
        pl.BlockSpec((1, 1, VD), lambda b, h, t: (mix_idx, 0, 0)),
    ]
    ctx_kv = [
        pl.BlockSpec((L, VD), lambda b, h, t: (ctx_blk + b, kcol + h)),
        pl.BlockSpec((L, VD), lambda b, h, t: (ctx_blk + b, vcol + h)),
    ]
    g3 = subln_g.reshape(-1, 1, VD)
    o = pl.pallas_call(
        functools.partial(_attn_body, lambda_init=lambda_init, with_lat=True),
        grid=(B, H, nqt),
        in_specs=common + [pl.BlockSpec((TQ, VD), lambda b, h, t: (b * nqt + t, h))] + ctx_kv + [
            pl.BlockSpec((S, VD), lambda b, h, t: (b, kcol + h)),
            pl.BlockSpec((S, VD), lambda b, h, t: (b, vcol + h)),
        ],
        out_specs=pl.BlockSpec((TQ, VD), lambda b, h, t: (b * nqt + t, h)),
        out_shape=jax.ShapeDtypeStruct((t_out, D), BF16),
        compiler_params=_params("arbitrary", "arbitrary", "arbitrary"),
        name="attn_lat",
    )(lam_p, g3, qkv, qkv, qkv, qkv, qkv)
    if not ctx_out:
        return o
    return pl.pallas_call(
        functools.partial(_attn_body, lambda_init=lambda_init, with_lat=False),
        grid=(B, H, 1),
        in_specs=common + [pl.BlockSpec((L, VD), lambda b, h, t: (ctx_blk + b, h))] + ctx_kv + [
            pl.BlockSpec(memory_space=pl.ANY),
        ],
        out_specs=pl.BlockSpec((L, VD), lambda b, h, t: (ctx_blk + b, h)),
        out_shape=jax.ShapeDtypeStruct((t_out, D), BF16),
        input_output_aliases={5: 0},
        compiler_params=_params("arbitrary", "arbitrary", "arbitrary"),
        name="attn_ctx",
    )(lam_p, g3, qkv, qkv, qkv, o)


def _resid_body(x_ref, w_ref, *rest, has_bias):
    if has_bias:
        b_ref, r_ref, gate_ref, o_ref = rest
    else:
        r_ref, gate_ref, o_ref = rest
    y = jnp.dot(x_ref[...], w_ref[...], preferred_element_type=F32)
    if has_bias:
        y = y + b_ref[0]
    o_ref[...] = r_ref[...] + gate_ref[0] * y


def _resid_proj(x, w, bias, resid, mods, layer, which, t_out, name):
    k = x.shape[1]
    has_bias = bias is not None
    in_specs = [pl.BlockSpec((TM, k), lambda i, j: (i, 0)),
                pl.BlockSpec((k, BN), lambda i, j: (0, j))]
    args = [x, w]
    if has_bias:
        idx, arr = bias
        in_specs.append(_layer_vec_spec(idx, BN, col=True))
        args.append(arr)
    in_specs += [pl.BlockSpec((TM, BN), lambda i, j: (i, j)), _mod_spec(layer, which, TM, BN, col=True)]
    args += [resid, mods]
    return pl.pallas_call(
        functools.partial(_resid_body, has_bias=has_bias),
        grid=(t_out // TM, D // BN),
        in_specs=in_specs,
        out_specs=pl.BlockSpec((TM, BN), lambda i, j: (i, j)),
        out_shape=jax.ShapeDtypeStruct((t_out, D), F32),
        compiler_params=_params("arbitrary", "arbitrary"),
        name=name,
    )(*args)


def _glu_body(x_ref, g_ref, sh_ref, sc_ref, wa_ref, wb_ref, *rest, kind):
    if kind == "glu":
        ba_ref, bb_ref, o_ref, hm_ref = rest
    else:
        o_ref, hm_ref = rest

    @pl.when(pl.program_id(1) == 0)
    def _():
        hm_ref[...] = _modulate(x_ref[...], g_ref[0], sh_ref[0], sc_ref[0]).astype(BF16)

    hm = hm_ref[...]
    a = jnp.dot(hm, wa_ref[...], preferred_element_type=F32)
    b = jnp.dot(hm, wb_ref[...], preferred_element_type=F32)
    if kind == "glu":
        o = (a + ba_ref[0]) * _sigmoid(b + bb_ref[0])
    else:
        o = (a * _sigmoid(a)) * b
    o_ref[...] = o.astype(o_ref.dtype)


def _glu_proj(h, mods, g, layer, w, bias, t_rows, kind, out_dtype, name):
    nh = w.shape[1] // 2
    nj = nh // BN
    in_specs = [
        pl.BlockSpec((TM, D), lambda i, j: (i, 0)),
        _layer_vec_spec(layer),
        _mod_spec(layer, 3 if kind == "swiglu" else 0, TM),
        _mod_spec(layer, 4 if kind == "swiglu" else 1, TM),
        pl.BlockSpec((D, BN), lambda i, j: (0, j)),
        pl.BlockSpec((D, BN), lambda i, j: (0, nj + j)),
    ]
    args = [h, g, mods, mods, w, w]
    if kind == "glu":
        idx, arr = bias
        in_specs += [pl.BlockSpec((1, 1, BN), lambda i, j: (idx, 0, j)),
                     pl.BlockSpec((1, 1, BN), lambda i, j: (idx, 0, nj + j))]
        args += [arr, arr]
    return pl.pallas_call(
        functools.partial(_glu_body, kind=kind),
        grid=(t_rows // TM, nj),
        in_specs=in_specs,
        out_specs=pl.BlockSpec((TM, BN), lambda i, j: (i, j)),
        out_shape=jax.ShapeDtypeStruct((t_rows, nh), out_dtype),
        scratch_shapes=[pltpu.VMEM((TM, D), BF16)],
        compiler_params=_params("arbitrary", "arbitrary"),
        name=name,
    )(*args)


def _conv_body(prev_ref, cur_ref, next_ref, w_ref, b_ref, lg_ref, lb_ref, o_ref, ext_ref, y_ref):
    i = pl.program_id(0)
    tiles_per_seq = S // TS
    is_ctx = i * TS >= T_LAT
    first = jnp.logical_or(is_ctx, i % tiles_per_seq == 0)
    last = jnp.logical_or(is_ctx, i % tiles_per_seq == tiles_per_seq - 1)
    ext_ref[0:HALO, :] = jnp.where(first, 0.0, prev_ref[...])
    ext_ref[HALO:HALO + TS, :] = cur_ref[...]
    ext_ref[HALO + TS:2 * HALO + TS, :] = jnp.where(last, 0.0, next_ref[...])

    rc = 64
    off = HALO - CPAD

    def chan(c, carry):
        cs = pl.ds(pl.multiple_of(c * LANES, LANES), LANES)
        bias = b_ref[0, :, cs]
        accs = [jnp.broadcast_to(bias, (rc, LANES)) for _ in range(TS // rc)]
        for k in range(CW):
            wk = jnp.broadcast_to(w_ref[0, k:k + 1, cs], (rc, LANES))
            for r in range(TS // rc):
                accs[r] = accs[r] + wk * ext_ref[pl.ds(r * rc + k + off, rc), cs]
        for r in range(TS // rc):
            y_ref[pl.ds(r * rc, rc), cs] = accs[r]
        return carry

    lax.fori_loop(0, D // LANES, chan, 0)

    y = y_ref[...]
    yc = y - jnp.mean(y, axis=-1, keepdims=True)
    var = jnp.mean(yc * yc, axis=-1, keepdims=True)
    z = yc * lax.rsqrt(var + EPS) * lg_ref[0] + lb_ref[0]
    o_ref[...] = (z * _sigmoid(z)).astype(BF16)


def _conv_module(u, w_dw, b_dw, ln_g, ln_b, idx, t_rows):
    assert L == TS and TS % HALO == 0 and HALO >= CPAD
    hb = TS // HALO
    n_halo_blocks = u.shape[0] // HALO
    return pl.pallas_call(
        _conv_body,
        grid=(t_rows // TS,),
        in_specs=[
            pl.BlockSpec((HALO, D), lambda i: (jnp.maximum(i * hb - 1, 0), 0)),
            pl.BlockSpec((TS, D), lambda i: (i, 0)),
            pl.BlockSpec((HALO, D), lambda i: (jnp.minimum((i + 1) * hb, n_halo_blocks - 1), 0)),
            pl.BlockSpec((1, CW, D), lambda i: (idx, 0, 0)),
            _layer_vec_spec(idx),
            _layer_vec_spec(idx),
            _layer_vec_spec(idx),
        ],
        out_specs=pl.BlockSpec((TS, D), lambda i: (i, 0)),
        out_shape=jax.ShapeDtypeStruct((t_rows, D), BF16),
        scratch_shapes=[pltpu.VMEM((TS + 2 * HALO, D), F32), pltpu.VMEM((TS, D), F32)],
        compiler_params=_params("arbitrary"),
        name="dwconv_ln_silu",
    )(u, u, u, w_dw, b_dw, ln_g, ln_b)


def _router_body(x_ref, g_ref, sh_ref, sc_ref, wr_ref, hm_ref, info_ref):
    hm = _modulate(x_ref[...], g_ref[0], sh_ref[0], sc_ref[0])
    hm_ref[...] = hm.astype(BF16)
    logits = jnp.dot(hm, wr_ref[...], precision=lax.Precision.HIGHEST, preferred_element_type=F32)
    lane = lax.broadcasted_iota(jnp.int32, logits.shape, 1).astype(F32)
    ninf = -jnp.inf
    lg = jnp.where(lane < E, logits, ninf)
    v1 = jnp.max(lg, axis=-1, keepdims=True)
    i1 = jnp.min(jnp.where(lg == v1, lane, float(LANES)), axis=-1, keepdims=True)
    lg2 = jnp.where(lane == i1, ninf, lg)
    v2 = jnp.max(lg2, axis=-1, keepdims=True)
    i2 = jnp.min(jnp.where(lg2 == v2, lane, float(LANES)), axis=-1, keepdims=True)
    e2 = jnp.exp(v2 - v1)
    w1 = 1.0 / (1.0 + e2)
    w2 = e2 * w1
    info_ref[...] = jnp.where(lane == 0, i1, jnp.where(lane == 1, i2,
                              jnp.where(lane == 2, w1, jnp.where(lane == 3, w2, 0.0))))


def _router(h, mods, g, layer, w_router_pad, t_rows):
    tm = 512
    return pl.pallas_call(
        _router_body,
        grid=(t_rows // tm,),
        in_specs=[
            pl.BlockSpec((tm, D), lambda i: (i, 0)),
            _layer_vec_spec(layer),
            _mod_spec(layer, 3, tm),
            _mod_spec(layer, 4, tm),
            pl.BlockSpec((D, LANES), lambda i: (0, 0)),
        ],
        out_specs=[pl.BlockSpec((tm, D), lambda i: (i, 0)),
                   pl.BlockSpec((tm, LANES), lambda i: (i, 0))],
        out_shape=[jax.ShapeDtypeStruct((t_rows, D), BF16),
                   jax.ShapeDtypeStruct((t_rows, LANES), F32)],
        compiler_params=_params("arbitrary"),
        name="router",
    )(h, g, mods, mods, w_router_pad)


def _permute_body(idx_ref, src_ref, dst_ref, sem):
    base = pl.program_id(0) * PERM_ROWS

    def row_copy(src_row, r):
        return pltpu.make_async_copy(src_ref.at[pl.ds(src_row, 1)], dst_ref.at[pl.ds(base + r, 1)], sem)

    def issue(r, carry):
        row_copy(idx_ref[0, 0, r], r).start()
        return carry

    def drain(r, carry):
        row_copy(0, r).wait()
        return carry

    lax.fori_loop(0, PERM_ROWS, issue, 0, unroll=8)
    lax.fori_loop(0, PERM_ROWS, drain, 0, unroll=8)


def _permute_rows(src, idx, name):
    n = idx.shape[0]
    assert n % PERM_ROWS == 0
    src = src.reshape(src.shape[0], D // LANES, LANES)
    out = pl.pallas_call(
        _permute_body,
        grid=(n // PERM_ROWS,),
        in_specs=[pl.BlockSpec((1, 1, PERM_ROWS), lambda i: (i, 0, 0), memory_space=pltpu.SMEM),
                  pl.BlockSpec(memory_space=pl.ANY)],
        out_specs=pl.BlockSpec(memory_space=pl.ANY),
        out_shape=jax.ShapeDtypeStruct((n, D // LANES, LANES), src.dtype),
        scratch_shapes=[pltpu.SemaphoreType.DMA(())],
        compiler_params=_params("arbitrary"),
        name=name,
    )(idx.reshape(n // PERM_ROWS, 1, PERM_ROWS), src)
    return out.reshape(n, D)


def _moe_up_body(te_ref, nu_ref, x_ref, wg_ref, wu_ref, o_ref):
    @pl.when(pl.program_id(1) < nu_ref[0])
    def _():
        x = x_ref[...]
        a = jnp.dot(x, wg_ref[0], preferred_element_type=F32)
        b = jnp.dot(x, wu_ref[0], preferred_element_type=F32)
        o_ref[...] = ((a * _sigmoid(a)) * b).astype(BF16)


def _moe_down_body(te_ref, nu_ref, a_ref, wd_ref, gp_ref, o_ref):
    @pl.when(pl.program_id(1) < nu_ref[0])
    def _():
        o_ref[...] = gp_ref[...] * jnp.dot(a_ref[...], wd_ref[0], preferred_element_type=F32)


def _moe_experts(xs, w_gu, w_down, tile_expert, n_used, gate_of_pos):
    n_tiles = xs.shape[0] // TME
    nj = DFF // BN_UP

    def tile(i, nu):
        return jnp.minimum(i, nu[0] - 1)

    act = pl.pallas_call(
        _moe_up_body,
        grid_spec=pltpu.PrefetchScalarGridSpec(
            num_scalar_prefetch=2,
            grid=(nj, n_tiles),
            in_specs=[
                pl.BlockSpec((TME, D), lambda j, i, te, nu: (tile(i, nu), 0)),
                pl.BlockSpec((1, D, BN_UP), lambda j, i, te, nu: (te[tile(i, nu)], 0, j)),
                pl.BlockSpec((1, D, BN_UP), lambda j, i, te, nu: (te[tile(i, nu)], 0, nj + j)),
            ],
            out_specs=pl.BlockSpec((TME, BN_UP), lambda j, i, te, nu: (tile(i, nu), j)),
        ),
        out_shape=jax.ShapeDtypeStruct((n_tiles * TME, DFF), BF16),
        compiler_params=_params("arbitrary", "arbitrary"),
        name="moe_up",
    )(tile_expert, n_used, xs, w_gu, w_gu)

    return pl.pallas_call(
        _moe_down_body,
        grid_spec=pltpu.PrefetchScalarGridSpec(
            num_scalar_prefetch=2,
            grid=(D // BN_DOWN, n_tiles),
            in_specs=[
                pl.BlockSpec((TME, DFF), lambda j, i, te, nu: (tile(i, nu), 0)),
                pl.BlockSpec((1, DFF, BN_DOWN), lambda j, i, te, nu: (te[tile(i, nu)], 0, j)),
                pl.BlockSpec((TME, 1), lambda j, i, te, nu: (tile(i, nu), 0)),
            ],
            out_specs=pl.BlockSpec((TME, BN_DOWN), lambda j, i, te, nu: (tile(i, nu), j)),
        ),
        out_shape=jax.ShapeDtypeStruct((n_tiles * TME, D), F32),
        compiler_params=_params("arbitrary", "arbitrary"),
        name="moe_down",
    )(tile_expert, n_used, act, w_down, gate_of_pos)


def _routing_tables(info, t_rows, n_tiles):
    idx = info[:, :2].astype(jnp.int32)
    flat_e = idx.T.reshape(-1)
    flat_w = info[:, 2:4].T.reshape(-1)
    onehot = (flat_e[:, None] == jnp.arange(E, dtype=jnp.int32)[None, :]).astype(jnp.int32)
    csum = jnp.cumsum(onehot, axis=0)
    rank = jnp.sum((csum - onehot) * onehot, axis=1)
    counts = csum[-1]
    padded = ((counts + TME - 1) // TME) * TME
    ends = jnp.cumsum(padded)
    starts = ends - padded
    pos = starts[flat_e] + rank
    n_rows = n_tiles * TME
    token_of_pos = jnp.zeros((n_rows,), jnp.int32).at[pos].set(
        jnp.arange(2 * t_rows, dtype=jnp.int32) % t_rows)
    gate_of_pos = jnp.zeros((n_rows,), F32).at[pos].set(flat_w).reshape(n_rows, 1)
    tile_start = jnp.arange(n_tiles, dtype=jnp.int32) * TME
    tile_expert = jnp.minimum(jnp.sum((tile_start[:, None] >= ends[None, :]).astype(jnp.int32), axis=1), E - 1)
    n_used = (ends[-1] // TME).astype(jnp.int32).reshape(1)
    return pos.astype(jnp.int32), token_of_pos, gate_of_pos, tile_expert.astype(jnp.int32), n_used


def _combine_body(h_ref, y0_ref, y1_ref, gate_ref, *rest, final):
    if final:
        fg_ref, o_ref = rest
    else:
        (o_ref,) = rest
    h = h_ref[...] + gate_ref[0] * (y0_ref[...] + y1_ref[...])
    if final:
        h = (h * lax.rsqrt(jnp.mean(h * h, axis=-1, keepdims=True) + EPS)) * fg_ref[...]
    o_ref[...] = h


def _combine(h, y2, mods, layer, t_rows, final_g):
    tm = 512
    nt = t_rows // tm
    final = final_g is not None
    in_specs = [pl.BlockSpec((tm, D), lambda i: (i, 0)),
                pl.BlockSpec((tm, D), lambda i: (i, 0)),
                pl.BlockSpec((tm, D), lambda i: (nt + i, 0)),
                _mod_spec(layer, 5, tm)]
    args = [h, y2, y2, mods]
    if final:
        in_specs.append(pl.BlockSpec((1, D), lambda i: (0, 0)))
        args.append(final_g.reshape(1, D))
    return pl.pallas_call(
        functools.partial(_combine_body, final=final),
        grid=(nt,),
        in_specs=in_specs,
        out_specs=pl.BlockSpec((tm, D), lambda i: (i, 0)),
        out_shape=jax.ShapeDtypeStruct((t_rows, D), F32),
        compiler_params=_params("arbitrary"),
        name="moe_combine",
    )(*args)


def _rope_tables():
    rows = S // GRID_W
    row = jnp.repeat(jnp.arange(rows, dtype=F32), GRID_W)
    col = jnp.tile(jnp.arange(GRID_W, dtype=F32), rows)
    n_freq = HD // 4
    inv_freq = ROPE_BASE ** (-jnp.arange(n_freq, dtype=F32) / n_freq)
    ang = jnp.concatenate([row[:, None] * inv_freq, col[:, None] * inv_freq], axis=-1)
    ang = jnp.concatenate([ang, ang], axis=-1)
    sign = jnp.where(jnp.arange(HD) < HD // 2, -1.0, 1.0).astype(F32)
    cos = jnp.concatenate([jnp.cos(ang), jnp.ones((TM, HD), F32)], axis=0)
    sin = jnp.concatenate([jnp.sin(ang) * sign, jnp.zeros((TM, HD), F32)], axis=0)
    scale = HD ** -0.5
    return jnp.stack([cos * scale, cos]), jnp.stack([sin * scale, sin])


def kernel(x, c, ctx, c_ctx, ada_w, ada_b, norm_mix_g, norm_ffn_g, attn_w_qkv, attn_lambda, attn_subln_g, attn_w_o, conv_w_in, conv_b_in, conv_w_dw, conv_b_dw, conv_ln_g, conv_ln_b, conv_w_out, conv_b_out, ffn_w_gu, ffn_w_down, moe_router, moe_w_gu, moe_w_down, final_g):
    h = jnp.concatenate([x.reshape(T_LAT, D), ctx.reshape(T_CTX, D)], axis=0)
    cond = jnp.concatenate([c, c_ctx[None, :], jnp.zeros((COND_ROWS - B - 1, D), F32)], axis=0)
    mods = _ada_mods(cond, ada_w, ada_b)
    cos_t, sin_t = _rope_tables()
    mix_g = norm_mix_g.reshape(DEPTH, 1, D)
    ffn_g = norm_ffn_g.reshape(DEPTH, 1, D)

    for i in range(DEPTH):
        mix_idx = i // 2
        ffn_idx = i // 2
        ctx_live = i < LAST_CTX_READER
        t_rows = T_ALL if ctx_live else T_LAT
        if i % 2 == 0:
            lambda_init = 0.8 - 0.6 * math.exp(-0.3 * i)
            qkv = _qkv_proj(h, mods, mix_g, i, attn_w_qkv[mix_idx].astype(BF16), cos_t, sin_t)
            o = _attention(qkv, attn_lambda, attn_subln_g, mix_idx, lambda_init, ctx_live)
            h = _resid_proj(o, attn_w_o[mix_idx].astype(BF16), None, h, mods, i, 2, t_rows, "attn_out")
        else:
            u = _glu_proj(h, mods, mix_g, i, conv_w_in[mix_idx].astype(BF16),
                          (mix_idx, conv_b_in.reshape(-1, 1, 2 * D)), t_rows, "glu", F32, "conv_in")
            z = _conv_module(u, conv_w_dw, conv_b_dw.reshape(-1, 1, D), conv_ln_g.reshape(-1, 1, D),
                             conv_ln_b.reshape(-1, 1, D), mix_idx, t_rows)
            h = _resid_proj(z, conv_w_out[mix_idx].astype(BF16), (mix_idx, conv_b_out.reshape(-1, 1, D)),
                            h, mods, i, 2, t_rows, "conv_out")
        if i % 2 == 0:
            act = _glu_proj(h, mods, ffn_g, i, ffn_w_gu[ffn_idx].astype(BF16), None, t_rows,
                            "swiglu", BF16, "ffn_up")
            h = _resid_proj(act, ffn_w_down[ffn_idx].astype(BF16), None, h, mods, i, 5, t_rows, "ffn_down")
        else:
            n_tiles = N_TILES_E if ctx_live else N_TILES_E_LAT
            w_r = jnp.zeros((D, LANES), F32).at[:, :E].set(moe_router[ffn_idx])
            hm, info = _router(h, mods, ffn_g, i, w_r, t_rows)
            pos, token_of_pos, gate_of_pos, tile_expert, n_used = _routing_tables(info, t_rows, n_tiles)
            xs = _permute_rows(hm, token_of_pos, "moe_gather")
            ys = _moe_experts(xs, moe_w_gu[ffn_idx].astype(BF16), moe_w_down[ffn_idx].astype(BF16),
                              tile_expert, n_used, gate_of_pos)
            y2 = _permute_rows(ys, pos, "moe_ungather")
            h = _combine(h, y2, mods, i, t_rows, final_g if i == DEPTH - 1 else None)
    return h.reshape(B, S, D)
```

```python
import functools
import math

import jax
import jax.numpy as jnp
from jax import lax
from jax.experimental import pallas as pl
from jax.experimental.pallas import tpu as pltpu

F32 = jnp.float32
BF16 = jnp.bfloat16

D = 2048
B = 16
S = 2048
L = 256
DEPTH = 4
GRID_W = 64
H = 8
HD = 128
VD = 2 * HD
ROPE_BASE = 10000.0
CW = 31
CPAD = (CW - 1) // 2
DFF = 5632
E = 8
NMOD = 6
EPS = 1e-6

T_LAT = B * S
T_CTX = B * L
T_ALL = T_LAT + T_CTX
COND_ROWS = 24
CTX_COND_ROW = B
LAST_CTX_READER = 2

V7X_VMEM_LIMIT_BYTES = 56 * 1024 * 1024
LANES = 128
HALO = 16

TM = 1024
BN = 512
TQ = 512
TS = 256
TME = 512
BN_UP = 512
SLABS = D // LANES
ROUTER_TM = 512
GATHER_ROWS = 256
COMBINE_TM = 256
N_TILES_E = (2 * T_ALL + E * (TME - 1) + TME - 1) // TME
N_TILES_E_LAT = (2 * T_LAT + E * (TME - 1) + TME - 1) // TME


def _params(*sem):
    return pltpu.CompilerParams(dimension_semantics=sem, vmem_limit_bytes=V7X_VMEM_LIMIT_BYTES)


def _sigmoid(x):
    return 1.0 / (1.0 + jnp.exp(-x))


def _modulate(x, g, shift, scale):
    xn = x * lax.rsqrt(jnp.mean(x * x, axis=-1, keepdims=True) + EPS)
    return (xn * g) * (1.0 + scale) + shift


def _cond_row(i, tm):
    return jnp.where(i * tm < T_LAT, (i * tm) // S, CTX_COND_ROW)


def _mod_spec(layer, which, tm, bn=D, col=False):
    def imap(i, *rest):
        j = rest[0] if col else 0
        return ((layer * COND_ROWS + _cond_row(i, tm)) * NMOD + which, 0, j)
    return pl.BlockSpec((1, 1, bn), imap)


def _layer_vec_spec(idx, bn=D, col=False):
    def imap(i, *rest):
        return (idx, 0, rest[0] if col else 0)
    return pl.BlockSpec((1, 1, bn), imap)


def _ada_body(c_ref, w_ref, b_ref, o_ref):
    x = c_ref[...]
    s = (x * _sigmoid(x)).astype(BF16)
    o_ref[0] = jnp.dot(s, w_ref[0].astype(BF16), preferred_element_type=F32) + b_ref[0]


def _ada_mods(cond, ada_w, ada_b):
    bn = 1024
    n = NMOD * D
    out = pl.pallas_call(
        _ada_body,
        grid=(DEPTH, n // bn),
        in_specs=[
            pl.BlockSpec((COND_ROWS, D), lambda l, j: (0, 0)),
            pl.BlockSpec((1, D, bn), lambda l, j: (l, 0, j)),
            pl.BlockSpec((1, 1, bn), lambda l, j: (l, 0, j)),
        ],
        out_specs=pl.BlockSpec((1, COND_ROWS, bn), lambda l, j: (l, 0, j)),
        out_shape=jax.ShapeDtypeStruct((DEPTH, COND_ROWS, n), F32),
        compiler_params=_params("arbitrary", "arbitrary"),
        name="ada_mods",
    )(cond, ada_w, ada_b.reshape(DEPTH, 1, n))
    return out.reshape(DEPTH * COND_ROWS * NMOD, 1, D)


def _qkv_body(x_ref, g_ref, sh_ref, sc_ref, w_ref, cos_ref, sin_ref, o_ref, hm_ref, *, n_rope):
    j = pl.program_id(1)

    @pl.when(j == 0)
    def _():
        hm_ref[...] = _modulate(x_ref[...], g_ref[0], sh_ref[0], sc_ref[0]).astype(BF16)

    acc = jnp.dot(hm_ref[...], w_ref[...], preferred_element_type=F32)

    @pl.when(j < n_rope)
    def _():
        cos = cos_ref[0]
        sin = sin_ref[0]
        for c in range(BN // HD):
            xc = acc[:, c * HD:(c + 1) * HD]
            o_ref[:, c * HD:(c + 1) * HD] = (xc * cos + pltpu.roll(xc, HD // 2, 1) * sin).astype(BF16)

    @pl.when(j >= n_rope)
    def _():
        o_ref[...] = acc.astype(BF16)


def _qkv_proj(h, mods, g, layer, w_qkv, cos_t, sin_t):
    nt = T_ALL // TM
    nq = D // BN
    tiles_per_seq = S // TM

    def rope_map(i, j):
        return (jnp.where(j < nq, 0, 1),
                jnp.where(i * TM < T_LAT, i % tiles_per_seq, tiles_per_seq), 0)

    return pl.pallas_call(
        functools.partial(_qkv_body, n_rope=2 * nq),
        grid=(nt, 3 * D // BN),
        in_specs=[
            pl.BlockSpec((TM, D), lambda i, j: (i, 0)),
            _layer_vec_spec(layer),
            _mod_spec(layer, 0, TM),
            _mod_spec(layer, 1, TM),
            pl.BlockSpec((D, BN), lambda i, j: (0, j)),
            pl.BlockSpec((1, TM, HD), rope_map),
            pl.BlockSpec((1, TM, HD), rope_map),
        ],
        out_specs=pl.BlockSpec((TM, BN), lambda i, j: (i, j)),
        out_shape=jax.ShapeDtypeStruct((T_ALL, 3 * D), BF16),
        scratch_shapes=[pltpu.VMEM((TM, D), BF16)],
        compiler_params=_params("arbitrary", "arbitrary"),
        name="qkv_proj",
    )(h, g, mods, mods, w_qkv, cos_t, sin_t)


def _attn_body(lam_ref, g_ref, q_ref, kc_ref, vc_ref, *rest, lambda_init, with_lat):
    if with_lat:
        k_ref, v_ref, o_ref = rest
    else:
        o_ref = rest[-1]
    lp = lam_ref[0]
    lam = (jnp.exp(jnp.sum(lp[0:1] * lp[1:2], axis=-1, keepdims=True))
           - jnp.exp(jnp.sum(lp[2:3] * lp[3:4], axis=-1, keepdims=True)) + lambda_init)
    q = q_ref[...]
    nt_dims = (((1,), (1,)), ((), ()))
    parts = []
    for c in range(2):
        sl = slice(c * HD, (c + 1) * HD)
        qc = q[:, sl]
        s_c = lax.dot_general(qc, kc_ref[:, sl], nt_dims, preferred_element_type=F32)
        m = jnp.max(s_c, axis=-1, keepdims=True)
        if with_lat:
            s_l = lax.dot_general(qc, k_ref[:, sl], nt_dims, preferred_element_type=F32)
            m = jnp.maximum(m, jnp.max(s_l, axis=-1, keepdims=True))
            p_l = jnp.exp(s_l - m)
        p_c = jnp.exp(s_c - m)
        den = jnp.sum(p_c, axis=-1, keepdims=True)
        if with_lat:
            den = den + jnp.sum(p_l, axis=-1, keepdims=True)
        parts.append((p_c, p_l if with_lat else None, den))
    (p1c, p1l, den1), (p2c, p2l, den2) = parts
    r = lam * den1 / den2
    o = jnp.dot((p1c - r * p2c).astype(BF16), vc_ref[...], preferred_element_type=F32)
    if with_lat:
        o = o + jnp.dot((p1l - r * p2l).astype(BF16), v_ref[...], preferred_element_type=F32)
    o = o / den1
    o = o * lax.rsqrt(jnp.mean(o * o, axis=-1, keepdims=True) + EPS) * g_ref[0] * (1.0 - lambda_init)
    o_ref[...] = o.astype(BF16)


def _attention(qkv, lam_p, subln_g, mix_idx, lambda_init, ctx_out):
    t_out = T_ALL if ctx_out else T_LAT
    nqt = S // TQ
    kcol, vcol = D // VD, 2 * D // VD
    ctx_blk = T_LAT // L
    common = [
        pl.BlockSpec((1, 4, HD), lambda b, h, t: (mix_idx, 0, 0)),
        pl.BlockSpec((1, 1, VD), lambda b, h, t: (mix_idx, 0, 0)),
    ]
    ctx_kv = [
        pl.BlockSpec((L, VD), lambda b, h, t: (ctx_blk + b, kcol + h)),
        pl.BlockSpec((L, VD), lambda b, h, t: (ctx_blk + b, vcol + h)),
    ]
    g3 = subln_g.reshape(-1, 1, VD)
    o = pl.pallas_call(
        functools.partial(_attn_body, lambda_init=lambda_init, with_lat=True),
        grid=(B, H, nqt),
        in_specs=common + [pl.BlockSpec((TQ, VD), lambda b, h, t: (b * nqt + t, h))] + ctx_kv + [
            pl.BlockSpec((S, VD), lambda b, h, t: (b, kcol + h)),
            pl.BlockSpec((S, VD), lambda b, h, t: (b, vcol + h)),
        ],
        out_specs=pl.BlockSpec((TQ, VD), lambda b, h, t: (b * nqt + t, h)),
        out_shape=jax.ShapeDtypeStruct((t_out, D), BF16),
        compiler_params=_params("arbitrary", "arbitrary", "arbitrary"),
        name="attn_lat",
    )(lam_p, g3, qkv, qkv, qkv, qkv, qkv)
    if not ctx_out:
        return o
    return pl.pallas_call(
        functools.partial(_attn_body, lambda_init=lambda_init, with_lat=False),
        grid=(B, H, 1),
        in_specs=common + [pl.BlockSpec((L, VD), lambda b, h, t: (ctx_blk + b, h))] + ctx_kv + [
            pl.BlockSpec(memory_space=pl.ANY),
        ],
        out_specs=pl.BlockSpec((L, VD), lambda b, h, t: (ctx_blk + b, h)),
        out_shape=jax.ShapeDtypeStruct((t_out, D), BF16),
        input_output_aliases={5: 0},
        compiler_params=_params("arbitrary", "arbitrary", "arbitrary"),
        name="attn_ctx",
    )(lam_p, g3, qkv, qkv, qkv, o)


def _resid_body(x_ref, w_ref, *rest, has_bias):
    if has_bias:
        b_ref, r_ref, gate_ref, o_ref = rest
    else:
        r_ref, gate_ref, o_ref = rest
    y = jnp.dot(x_ref[...], w_ref[...], preferred_element_type=F32)
    if has_bias:
        y = y + b_ref[0]
    o_ref[...] = r_ref[...] + gate_ref[0] * y


def _resid_proj(x, w, bias, resid, mods, layer, which, t_out, name):
    k = x.shape[1]
    has_bias = bias is not None
    in_specs = [pl.BlockSpec((TM, k), lambda i, j: (i, 0)),
                pl.BlockSpec((k, BN), lambda i, j: (0, j))]
    args = [x, w]
    if has_bias:
        idx, arr = bias
        in_specs.append(_layer_vec_spec(idx, BN, col=True))
        args.append(arr)
    in_specs += [pl.BlockSpec((TM, BN), lambda i, j: (i, j)), _mod_spec(layer, which, TM, BN, col=True)]
    args += [resid, mods]
    return pl.pallas_call(
        functools.partial(_resid_body, has_bias=has_bias),
        grid=(t_out // TM, D // BN),
        in_specs=in_specs,
        out_specs=pl.BlockSpec((TM, BN), lambda i, j: (i, j)),
        out_shape=jax.ShapeDtypeStruct((t_out, D), F32),
        compiler_params=_params("arbitrary", "arbitrary"),
        name=name,
    )(*args)


def _glu_body(x_ref, g_ref, sh_ref, sc_ref, wa_ref, wb_ref, *rest, kind):
    if kind == "glu":
        ba_ref, bb_ref, o_ref, hm_ref = rest
    else:
        o_ref, hm_ref = rest

    @pl.when(pl.program_id(1) == 0)
    def _():
        hm_ref[...] = _modulate(x_ref[...], g_ref[0], sh_ref[0], sc_ref[0]).astype(BF16)

    hm = hm_ref[...]
    a = jnp.dot(hm, wa_ref[...], preferred_element_type=F32)
    b = jnp.dot(hm, wb_ref[...], preferred_element_type=F32)
    if kind == "glu":
        o = (a + ba_ref[0]) * _sigmoid(b + bb_ref[0])
    else:
        o = (a * _sigmoid(a)) * b
    o_ref[...] = o.astype(o_ref.dtype)


def _glu_proj(h, mods, g, layer, w, bias, t_rows, kind, out_dtype, name):
    nh = w.shape[1] // 2
    nj = nh // BN
    in_specs = [
        pl.BlockSpec((TM, D), lambda i, j: (i, 0)),
        _layer_vec_spec(layer),
        _mod_spec(layer, 3 if kind == "swiglu" else 0, TM),
        _mod_spec(layer, 4 if kind == "swiglu" else 1, TM),
        pl.BlockSpec((D, BN), lambda i, j: (0, j)),
        pl.BlockSpec((D, BN), lambda i, j: (0, nj + j)),
    ]
    args = [h, g, mods, mods, w, w]
    if kind == "glu":
        idx, arr = bias
        in_specs += [pl.BlockSpec((1, 1, BN), lambda i, j: (idx, 0, j)),
                     pl.BlockSpec((1, 1, BN), lambda i, j: (idx, 0, nj + j))]
        args += [arr, arr]
    return pl.pallas_call(
        functools.partial(_glu_body, kind=kind),
        grid=(t_rows // TM, nj),
        in_specs=in_specs,
        out_specs=pl.BlockSpec((TM, BN), lambda i, j: (i, j)),
        out_shape=jax.ShapeDtypeStruct((t_rows, nh), out_dtype),
        scratch_shapes=[pltpu.VMEM((TM, D), BF16)],
        compiler_params=_params("arbitrary", "arbitrary"),
        name=name,
    )(*args)


def _conv_body(prev_ref, cur_ref, next_ref, w_ref, b_ref, lg_ref, lb_ref, o_ref, ext_ref, y_ref):
    i = pl.program_id(0)
    tiles_per_seq = S // TS
    is_ctx = i * TS >= T_LAT
    first = jnp.logical_or(is_ctx, i % tiles_per_seq == 0)
    last = jnp.logical_or(is_ctx, i % tiles_per_seq == tiles_per_seq - 1)
    ext_ref[0:HALO, :] = jnp.where(first, 0.0, prev_ref[...])
    ext_ref[HALO:HALO + TS, :] = cur_ref[...]
    ext_ref[HALO + TS:2 * HALO + TS, :] = jnp.where(last, 0.0, next_ref[...])

    rc = 64
    off = HALO - CPAD

    def chan(c, carry):
        cs = pl.ds(pl.multiple_of(c * LANES, LANES), LANES)
        bias = b_ref[0, :, cs]
        accs = [jnp.broadcast_to(bias, (rc, LANES)) for _ in range(TS // rc)]
        for k in range(CW):
            wk = jnp.broadcast_to(w_ref[0, k:k + 1, cs], (rc, LANES))
            for r in range(TS // rc):
                accs[r] = accs[r] + wk * ext_ref[pl.ds(r * rc + k + off, rc), cs]
        for r in range(TS // rc):
            y_ref[pl.ds(r * rc, rc), cs] = accs[r]
        return carry

    lax.fori_loop(0, D // LANES, chan, 0)

    y = y_ref[...]
    yc = y - jnp.mean(y, axis=-1, keepdims=True)
    var = jnp.mean(yc * yc, axis=-1, keepdims=True)
    z = yc * lax.rsqrt(var + EPS) * lg_ref[0] + lb_ref[0]
    o_ref[...] = (z * _sigmoid(z)).astype(BF16)


def _conv_module(u, w_dw, b_dw, ln_g, ln_b, idx, t_rows):
    assert L == TS and TS % HALO == 0 and HALO >= CPAD
    hb = TS // HALO
    n_halo_blocks = u.shape[0] // HALO
    return pl.pallas_call(
        _conv_body,
        grid=(t_rows // TS,),
        in_specs=[
            pl.BlockSpec((HALO, D), lambda i: (jnp.maximum(i * hb - 1, 0), 0)),
            pl.BlockSpec((TS, D), lambda i: (i, 0)),
            pl.BlockSpec((HALO, D), lambda i: (jnp.minimum((i + 1) * hb, n_halo_blocks - 1), 0)),
            pl.BlockSpec((1, CW, D), lambda i: (idx, 0, 0)),
            _layer_vec_spec(idx),
            _layer_vec_spec(idx),
            _layer_vec_spec(idx),
        ],
        out_specs=pl.BlockSpec((TS, D), lambda i: (i, 0)),
        out_shape=jax.ShapeDtypeStruct((t_rows, D), BF16),
        scratch_shapes=[pltpu.VMEM((TS + 2 * HALO, D), F32), pltpu.VMEM((TS, D), F32)],
        compiler_params=_params("arbitrary"),
        name="dwconv_ln_silu",
    )(u, u, u, w_dw, b_dw, ln_g, ln_b)


def _router_body(x_ref, g_ref, sh_ref, sc_ref, wr_ref, hm_ref, info_ref):
    hm = _modulate(x_ref[...], g_ref[0], sh_ref[0], sc_ref[0])
    for s in range(SLABS):
        hm_ref[pl.ds(s, ROUTER_TM, stride=SLABS), :] = hm[:, s * LANES:(s + 1) * LANES]
    logits = jnp.dot(hm, wr_ref[...], precision=lax.Precision.HIGHEST, preferred_element_type=F32)
    lane = lax.broadcasted_iota(jnp.int32, logits.shape, 1).astype(F32)
    ninf = -jnp.inf
    lg = jnp.where(lane < E, logits, ninf)
    v1 = jnp.max(lg, axis=-1, keepdims=True)
    i1 = jnp.min(jnp.where(lg == v1, lane, float(LANES)), axis=-1, keepdims=True)
    lg2 = jnp.where(lane == i1, ninf, lg)
    v2 = jnp.max(lg2, axis=-1, keepdims=True)
    i2 = jnp.min(jnp.where(lg2 == v2, lane, float(LANES)), axis=-1, keepdims=True)
    e2 = jnp.exp(v2 - v1)
    w1 = 1.0 / (1.0 + e2)
    w2 = e2 * w1
    info_ref[...] = jnp.where(lane == 0, i1, jnp.where(lane == 1, i2,
                              jnp.where(lane == 2, w1, jnp.where(lane == 3, w2, 0.0))))


def _router(h, mods, g, layer, w_router_pad, t_rows):
    tm = ROUTER_TM
    return pl.pallas_call(
        _router_body,
        grid=(t_rows // tm,),
        in_specs=[
            pl.BlockSpec((tm, D), lambda i: (i, 0)),
            _layer_vec_spec(layer),
            _mod_spec(layer, 3, tm),
            _mod_spec(layer, 4, tm),
            pl.BlockSpec((D, LANES), lambda i: (0, 0)),
        ],
        out_specs=[pl.BlockSpec((tm * SLABS, LANES), lambda i: (i, 0)),
                   pl.BlockSpec((tm, LANES), lambda i: (i, 0))],
        out_shape=[jax.ShapeDtypeStruct((t_rows * SLABS, LANES), F32),
                   jax.ShapeDtypeStruct((t_rows, LANES), F32)],
        compiler_params=_params("arbitrary"),
        name="router",
    )(h, g, mods, mods, w_router_pad)


def _row_gather_step(i, n_steps, idx_ref, idx_next_ref, src_ref, buf, sem, rows, consume):
    def row_copy(src_row, r, slot):
        return pltpu.make_async_copy(
            src_ref.at[pl.ds(pl.multiple_of(src_row * SLABS, SLABS), SLABS)],
            buf.at[slot, pl.ds(pl.multiple_of(r * SLABS, SLABS), SLABS)],
            sem.at[slot])

    def issue(ref, slot):
        def body(r, carry):
            row_copy(ref[0, 0, r], r, slot).start()
            return carry
        lax.fori_loop(0, rows, body, 0, unroll=8)

    def step(slot):
        if slot == 0:
            @pl.when(i == 0)
            def _():
                issue(idx_ref, 0)

        @pl.when(i + 1 < n_steps)
        def _():
            issue(idx_next_ref, 1 - slot)

        pltpu.make_async_copy(src_ref.at[pl.ds(0, rows * SLABS)], buf.at[slot], sem.at[slot]).wait()
        consume(slot)

    @pl.when(i % 2 == 0)
    def _():
        step(0)

    @pl.when(i % 2 == 1)
    def _():
        step(1)


def _gather_body(idx_ref, idx_next_ref, src_ref, o_ref, buf, sem):
    def consume(slot):
        for s in range(SLABS):
            o_ref[:, s * LANES:(s + 1) * LANES] = buf[slot, pl.ds(s, GATHER_ROWS, stride=SLABS), :].astype(BF16)

    _row_gather_step(pl.program_id(0), pl.num_programs(0), idx_ref, idx_next_ref, src_ref, buf, sem,
                     GATHER_ROWS, consume)


def _gather_rows(src_slabs, idx):
    n = idx.shape[0]
    assert n % GATHER_ROWS == 0
    nsteps = n // GATHER_ROWS
    idx3 = idx.reshape(nsteps, 1, GATHER_ROWS)
    return pl.pallas_call(
        _gather_body,
        grid=(nsteps,),
        in_specs=[pl.BlockSpec((1, 1, GATHER_ROWS), lambda i: (i, 0, 0), memory_space=pltpu.SMEM),
                  pl.BlockSpec((1, 1, GATHER_ROWS), lambda i: (jnp.minimum(i + 1, nsteps - 1), 0, 0),
                               memory_space=pltpu.SMEM),
                  pl.BlockSpec(memory_space=pl.ANY)],
        out_specs=pl.BlockSpec((GATHER_ROWS, D), lambda i: (i, 0)),
        out_shape=jax.ShapeDtypeStruct((n, D), BF16),
        scratch_shapes=[pltpu.VMEM((2, GATHER_ROWS * SLABS, LANES), F32), pltpu.SemaphoreType.DMA((2,))],
        compiler_params=_params("arbitrary"),
        name="moe_gather",
    )(idx3, idx3, src_slabs)


def _moe_up_body(te_ref, nu_ref, x_ref, wg_ref, wu_ref, o_ref, wg_bf, wu_bf):
    i = pl.program_id(1)

    @pl.when(i < nu_ref[0])
    def _():
        @pl.when(jnp.logical_or(i == 0, te_ref[i] != te_ref[jnp.maximum(i - 1, 0)]))
        def _():
            wg_bf[...] = wg_ref[0, 0].astype(BF16)
            wu_bf[...] = wu_ref[0, 0].astype(BF16)

        x = x_ref[...]
        a = jnp.dot(x, wg_bf[...], preferred_element_type=F32)
        b = jnp.dot(x, wu_bf[...], preferred_element_type=F32)
        o_ref[...] = ((a * _sigmoid(a)) * b).astype(BF16)


def _moe_down_body(te_ref, nu_ref, a_ref, wd_ref, o_ref):
    @pl.when(pl.program_id(0) < nu_ref[0])
    def _():
        y = jnp.dot(a_ref[...], wd_ref[0], preferred_element_type=F32)
        for s in range(SLABS):
            o_ref[pl.ds(s, TME, stride=SLABS), :] = y[:, s * LANES:(s + 1) * LANES]


def _moe_experts(xs, w_gu_all, moe_idx, w_down, tile_expert, n_used):
    n_tiles = xs.shape[0] // TME
    nj = DFF // BN_UP

    def tile(i, nu):
        return jnp.minimum(i, nu[0] - 1)

    act = pl.pallas_call(
        _moe_up_body,
        grid_spec=pltpu.PrefetchScalarGridSpec(
            num_scalar_prefetch=2,
            grid=(nj, n_tiles),
            in_specs=[
                pl.BlockSpec((TME, D), lambda j, i, te, nu: (tile(i, nu), 0)),
                pl.BlockSpec((1, 1, D, BN_UP), lambda j, i, te, nu: (moe_idx, te[tile(i, nu)], 0, j)),
                pl.BlockSpec((1, 1, D, BN_UP), lambda j, i, te, nu: (moe_idx, te[tile(i, nu)], 0, nj + j)),
            ],
            out_specs=pl.BlockSpec((TME, BN_UP), lambda j, i, te, nu: (tile(i, nu), j)),
            scratch_shapes=[pltpu.VMEM((D, BN_UP), BF16), pltpu.VMEM((D, BN_UP), BF16)],
        ),
        out_shape=jax.ShapeDtypeStruct((n_tiles * TME, DFF), BF16),
        compiler_params=_params("arbitrary", "arbitrary"),
        name="moe_up",
    )(tile_expert, n_used, xs, w_gu_all, w_gu_all)

    return pl.pallas_call(
        _moe_down_body,
        grid_spec=pltpu.PrefetchScalarGridSpec(
            num_scalar_prefetch=2,
            grid=(n_tiles,),
            in_specs=[
                pl.BlockSpec((TME, DFF), lambda i, te, nu: (tile(i, nu), 0)),
                pl.BlockSpec((1, DFF, D), lambda i, te, nu: (te[tile(i, nu)], 0, 0),
                             pipeline_mode=pl.Buffered(1)),
            ],
            out_specs=pl.BlockSpec((TME * SLABS, LANES), lambda i, te, nu: (tile(i, nu), 0)),
        ),
        out_shape=jax.ShapeDtypeStruct((n_tiles * TME * SLABS, LANES), F32),
        compiler_params=_params("arbitrary"),
        name="moe_down",
    )(tile_expert, n_used, act, w_down)


def _routing_tables(info, t_rows, n_tiles):
    idx = info[:, :2].astype(jnp.int32)
    flat_e = idx.T.reshape(-1)
    onehot = (flat_e[:, None] == jnp.arange(E, dtype=jnp.int32)[None, :]).astype(jnp.int32)
    csum = jnp.cumsum(onehot, axis=0)
    rank = jnp.sum((csum - onehot) * onehot, axis=1)
    counts = csum[-1]
    padded = ((counts + TME - 1) // TME) * TME
    ends = jnp.cumsum(padded)
    starts = ends - padded
    pos = starts[flat_e] + rank
    n_rows = n_tiles * TME
    tile_start = jnp.arange(n_tiles, dtype=jnp.int32) * TME
    tile_expert = jnp.minimum(jnp.sum((tile_start[:, None] >= ends[None, :]).astype(jnp.int32), axis=1), E - 1)
    n_used = (ends[-1] // TME).astype(jnp.int32).reshape(1)
    token_of_pos = jnp.zeros((n_rows,), jnp.int32).at[pos].set(
        jnp.arange(2 * t_rows, dtype=jnp.int32) % t_rows)
    return pos.astype(jnp.int32), token_of_pos, tile_expert.astype(jnp.int32), n_used


def _combine_body(idx_ref, idx_next_ref, ys_ref, h_ref, info_ref, gate_ref, *rest, final):
    if final:
        fg_ref, o_ref, buf, sem = rest
    else:
        o_ref, buf, sem = rest

    def consume(slot):
        w0 = jnp.broadcast_to(info_ref[:, 2:3], (COMBINE_TM, LANES))
        w1 = jnp.broadcast_to(info_ref[:, 3:4], (COMBINE_TM, LANES))
        for s in range(SLABS):
            cols = slice(s * LANES, (s + 1) * LANES)
            y = (w0 * buf[slot, pl.ds(s, COMBINE_TM, stride=SLABS), :]
                 + w1 * buf[slot, pl.ds(COMBINE_TM * SLABS + s, COMBINE_TM, stride=SLABS), :])
            o_ref[:, cols] = h_ref[:, cols] + gate_ref[0, :, cols] * y
        if final:
            h = o_ref[...]
            o_ref[...] = (h * lax.rsqrt(jnp.mean(h * h, axis=-1, keepdims=True) + EPS)) * fg_ref[...]

    _row_gather_step(pl.program_id(0), pl.num_programs(0), idx_ref, idx_next_ref, ys_ref, buf, sem,
                     2 * COMBINE_TM, consume)


def _combine(h, ys_slabs, pos, info, mods, layer, t_rows, final_g):
    tm = COMBINE_TM
    nt = t_rows // tm
    final = final_g is not None
    idx3 = pos.reshape(2, nt, tm).transpose(1, 0, 2).reshape(nt, 1, 2 * tm)
    in_specs = [pl.BlockSpec((1, 1, 2 * tm), lambda i: (i, 0, 0), memory_space=pltpu.SMEM),
                pl.BlockSpec((1, 1, 2 * tm), lambda i: (jnp.minimum(i + 1, nt - 1), 0, 0),
                             memory_space=pltpu.SMEM),
                pl.BlockSpec(memory_space=pl.ANY),
                pl.BlockSpec((tm, D), lambda i: (i, 0)),
                pl.BlockSpec((tm, LANES), lambda i: (i, 0)),
                _mod_spec(layer, 5, tm)]
    args = [idx3, idx3, ys_slabs, h, info, mods]
    if final:
        in_specs.append(pl.BlockSpec((1, D), lambda i: (0, 0)))
        args.append(final_g.reshape(1, D))
    return pl.pallas_call(
        functools.partial(_combine_body, final=final),
        grid=(nt,),
        in_specs=in_specs,
        out_specs=pl.BlockSpec((tm, D), lambda i: (i, 0)),
        out_shape=jax.ShapeDtypeStruct((t_rows, D), F32),
        scratch_shapes=[pltpu.VMEM((2, 2 * tm * SLABS, LANES), F32), pltpu.SemaphoreType.DMA((2,))],
        compiler_params=_params("arbitrary"),
        name="moe_combine",
    )(*args)


def _rope_tables():
    rows = S // GRID_W
    row = jnp.repeat(jnp.arange(rows, dtype=F32), GRID_W)
    col = jnp.tile(jnp.arange(GRID_W, dtype=F32), rows)
    n_freq = HD // 4
    inv_freq = ROPE_BASE ** (-jnp.arange(n_freq, dtype=F32) / n_freq)
    ang = jnp.concatenate([row[:, None] * inv_freq, col[:, None] * inv_freq], axis=-1)
    ang = jnp.concatenate([ang, ang], axis=-1)
    sign = jnp.where(jnp.arange(HD) < HD // 2, -1.0, 1.0).astype(F32)
    cos = jnp.concatenate([jnp.cos(ang), jnp.ones((TM, HD), F32)], axis=0)
    sin = jnp.concatenate([jnp.sin(ang) * sign, jnp.zeros((TM, HD), F32)], axis=0)
    scale = HD ** -0.5
    return jnp.stack([cos * scale, cos]), jnp.stack([sin * scale, sin])


def kernel(x, c, ctx, c_ctx, ada_w, ada_b, norm_mix_g, norm_ffn_g, attn_w_qkv, attn_lambda, attn_subln_g, attn_w_o, conv_w_in, conv_b_in, conv_w_dw, conv_b_dw, conv_ln_g, conv_ln_b, conv_w_out, conv_b_out, ffn_w_gu, ffn_w_down, moe_router, moe_w_gu, moe_w_down, final_g):
    h = jnp.concatenate([x.reshape(T_LAT, D), ctx.reshape(T_CTX, D)], axis=0)
    cond = jnp.concatenate([c, c_ctx[None, :], jnp.zeros((COND_ROWS - B - 1, D), F32)], axis=0)
    mods = _ada_mods(cond, ada_w, ada_b)
    cos_t, sin_t = _rope_tables()
    mix_g = norm_mix_g.reshape(DEPTH, 1, D)
    ffn_g = norm_ffn_g.reshape(DEPTH, 1, D)

    for i in range(DEPTH):
        mix_idx = i // 2
        ffn_idx = i // 2
        ctx_live = i < LAST_CTX_READER
        t_rows = T_ALL if ctx_live else T_LAT
        if i % 2 == 0:
            lambda_init = 0.8 - 0.6 * math.exp(-0.3 * i)
            qkv = _qkv_proj(h, mods, mix_g, i, attn_w_qkv[mix_idx].astype(BF16), cos_t, sin_t)
            o = _attention(qkv, attn_lambda, attn_subln_g, mix_idx, lambda_init, ctx_live)
            h = _resid_proj(o, attn_w_o[mix_idx].astype(BF16), None, h, mods, i, 2, t_rows, "attn_out")
        else:
            u = _glu_proj(h, mods, mix_g, i, conv_w_in[mix_idx].astype(BF16),
                          (mix_idx, conv_b_in.reshape(-1, 1, 2 * D)), t_rows, "glu", F32, "conv_in")
            z = _conv_module(u, conv_w_dw, conv_b_dw.reshape(-1, 1, D), conv_ln_g.reshape(-1, 1, D),
                             conv_ln_b.reshape(-1, 1, D), mix_idx, t_rows)
            h = _resid_proj(z, conv_w_out[mix_idx].astype(BF16), (mix_idx, conv_b_out.reshape(-1, 1, D)),
                            h, mods, i, 2, t_rows, "conv_out")
        if i % 2 == 0:
            act = _glu_proj(h, mods, ffn_g, i, ffn_w_gu[ffn_idx].astype(BF16), None, t_rows,
                            "swiglu", BF16, "ffn_up")
            h = _resid_proj(act, ffn_w_down[ffn_idx].astype(BF16), None, h, mods, i, 5, t_rows, "ffn_down")
        else:
            n_tiles = N_TILES_E if ctx_live else N_TILES_E_LAT
            w_r = jnp.zeros((D, LANES), F32).at[:, :E].set(moe_router[ffn_idx])
            hm_slabs, info = _router(h, mods, ffn_g, i, w_r, t_rows)
            pos, token_of_pos, tile_expert, n_used = _routing_tables(info, t_rows, n_tiles)
            xs = _gather_rows(hm_slabs, token_of_pos)
            ys_slabs = _moe_experts(xs, moe_w_gu, ffn_idx, moe_w_down[ffn_idx].astype(BF16),
                                    tile_expert, n_used)
            h = _combine(h, ys_slabs, pos, info, mods, i, t_rows, final_g if i == DEPTH - 1 else None)
    return h.reshape(B, S, D)
```

```python
import functools
import math

import jax
import jax.numpy as jnp
from jax import lax
from jax.experimental import pallas as pl
from jax.experimental.pallas import tpu as pltpu

F32 = jnp.float32
BF16 = jnp.bfloat16

D = 2048
B = 16
S = 2048
L = 256
DEPTH = 4
GRID_W = 64
H = 8
HD = 128
VD = 2 * HD
ROPE_BASE = 10000.0
CW = 31
CPAD = (CW - 1) // 2
DFF = 5632
E = 8
NMOD = 6
EPS = 1e-6

T_LAT = B * S
T_CTX = B * L
T_ALL = T_LAT + T_CTX
COND_ROWS = 24
CTX_COND_ROW = B
LAST_CTX_READER = 2

V7X_VMEM_LIMIT_BYTES = 56 * 1024 * 1024
LANES = 128
SUBLANES = 8
HALO = 16
SH_EXTRA = SUBLANES * ((CW - 1 + HALO - CPAD) // SUBLANES)

TM = 1024
BN = 512
BN_QKV = 1024
TM_RESID = 512
TQ = 512
TS = 256
MXU_COLS = 256
TME = 1024
TME_UP = TME
TME_DOWN = 512
BN_UP = 512
SLABS = D // LANES
ROUTER_TM = 512
GATHER_ROWS = 256
COMBINE_TM = 256
N_TILES_E = (2 * T_ALL + E * (TME - 1) + TME - 1) // TME
N_TILES_E_LAT = (2 * T_LAT + E * (TME - 1) + TME - 1) // TME


def _params(*sem):
    return pltpu.CompilerParams(dimension_semantics=sem, vmem_limit_bytes=V7X_VMEM_LIMIT_BYTES)


def _sigmoid(x):
    return 1.0 / (1.0 + jnp.exp(-x))


def _modulate(x, g, shift, scale):
    xn = x * lax.rsqrt(jnp.mean(x * x, axis=-1, keepdims=True) + EPS)
    return (xn * g) * (1.0 + scale) + shift


def _cond_row(i, tm):
    return jnp.where(i * tm < T_LAT, (i * tm) // S, CTX_COND_ROW)


def _mod_spec(layer, which, tm, bn=D, col=False):
    def imap(i, *rest):
        j = rest[0] if col else 0
        return ((layer * COND_ROWS + _cond_row(i, tm)) * NMOD + which, 0, j)
    return pl.BlockSpec((1, 1, bn), imap)


def _layer_vec_spec(idx, bn=D, col=False):
    def imap(i, *rest):
        return (idx, 0, rest[0] if col else 0)
    return pl.BlockSpec((1, 1, bn), imap)


def _ada_body(c_ref, w_ref, b_ref, o_ref):
    x = c_ref[...]
    s = (x * _sigmoid(x)).astype(BF16)
    o_ref[0] = jnp.dot(s, w_ref[0].astype(BF16), preferred_element_type=F32) + b_ref[0]


def _ada_mods(cond, ada_w, ada_b):
    bn = 1024
    n = NMOD * D
    out = pl.pallas_call(
        _ada_body,
        grid=(DEPTH, n // bn),
        in_specs=[
            pl.BlockSpec((COND_ROWS, D), lambda l, j: (0, 0)),
            pl.BlockSpec((1, D, bn), lambda l, j: (l, 0, j)),
            pl.BlockSpec((1, 1, bn), lambda l, j: (l, 0, j)),
        ],
        out_specs=pl.BlockSpec((1, COND_ROWS, bn), lambda l, j: (l, 0, j)),
        out_shape=jax.ShapeDtypeStruct((DEPTH, COND_ROWS, n), F32),
        compiler_params=_params("arbitrary", "arbitrary"),
        name="ada_mods",
    )(cond, ada_w, ada_b.reshape(DEPTH, 1, n))
    return out.reshape(DEPTH * COND_ROWS * NMOD, 1, D)


def _qkv_body(x_ref, g_ref, sh_ref, sc_ref, w_ref, cos_ref, sin_ref, o_ref, hm_ref, *, n_rope):
    j = pl.program_id(1)

    @pl.when(j == 0)
    def _():
        hm_ref[...] = _modulate(x_ref[...], g_ref[0], sh_ref[0], sc_ref[0]).astype(BF16)

    hm = hm_ref[...]

    def chunk(c):
        return jnp.dot(hm, w_ref[:, c * MXU_COLS:(c + 1) * MXU_COLS], preferred_element_type=F32)

    @pl.when(j < n_rope)
    def _():
        cos = cos_ref[0]
        sin = sin_ref[0]
        for c in range(BN_QKV // MXU_COLS):
            acc = chunk(c)
            for k in range(MXU_COLS // HD):
                xc = acc[:, k * HD:(k + 1) * HD]
                col = c * MXU_COLS + k * HD
                o_ref[:, col:col + HD] = (xc * cos + pltpu.roll(xc, HD // 2, 1) * sin).astype(BF16)

    @pl.when(j >= n_rope)
    def _():
        for c in range(BN_QKV // MXU_COLS):
            o_ref[:, c * MXU_COLS:(c + 1) * MXU_COLS] = chunk(c).astype(BF16)


def _qkv_proj(h, mods, g, layer, w_qkv, cos_t, sin_t):
    nt = T_ALL // TM
    nq = D // BN_QKV
    tiles_per_seq = S // TM

    def rope_map(i, j):
        return (jnp.where(j < nq, 0, 1),
                jnp.where(i * TM < T_LAT, i % tiles_per_seq, tiles_per_seq), 0)

    return pl.pallas_call(
        functools.partial(_qkv_body, n_rope=2 * nq),
        grid=(nt, 3 * D // BN_QKV),
        in_specs=[
            pl.BlockSpec((TM, D), lambda i, j: (i, 0)),
            _layer_vec_spec(layer),
            _mod_spec(layer, 0, TM),
            _mod_spec(layer, 1, TM),
            pl.BlockSpec((D, BN_QKV), lambda i, j: (0, j)),
            pl.BlockSpec((1, TM, HD), rope_map),
            pl.BlockSpec((1, TM, HD), rope_map),
        ],
        out_specs=pl.BlockSpec((TM, BN_QKV), lambda i, j: (i, j)),
        out_shape=jax.ShapeDtypeStruct((T_ALL, 3 * D), BF16),
        scratch_shapes=[pltpu.VMEM((TM, D), BF16)],
        compiler_params=_params("arbitrary", "arbitrary"),
        name="qkv_proj",
    )(h, g, mods, mods, w_qkv, cos_t, sin_t)


def _attn_body(lam_ref, g_ref, q_ref, kc_ref, vc_ref, *rest, lambda_init, with_lat):
    if with_lat:
        k_ref, v_ref, o_ref = rest
    else:
        o_ref = rest[-1]
    lp = lam_ref[0]
    lam = (jnp.exp(jnp.sum(lp[0:1] * lp[1:2], axis=-1, keepdims=True))
           - jnp.exp(jnp.sum(lp[2:3] * lp[3:4], axis=-1, keepdims=True)) + lambda_init)
    q = q_ref[...]
    nt_dims = (((1,), (1,)), ((), ()))
    parts = []
    for c in range(2):
        sl = slice(c * HD, (c + 1) * HD)
        qc = q[:, sl]
        s_c = lax.dot_general(qc, kc_ref[:, sl], nt_dims, preferred_element_type=F32)
        m = jnp.max(s_c, axis=-1, keepdims=True)
        if with_lat:
            s_l = lax.dot_general(qc, k_ref[:, sl], nt_dims, preferred_element_type=F32)
            m = jnp.maximum(m, jnp.max(s_l, axis=-1, keepdims=True))
            p_l = jnp.exp2(s_l - m)
        p_c = jnp.exp2(s_c - m)
        den = jnp.sum(p_c, axis=-1, keepdims=True)
        if with_lat:
            den = den + jnp.sum(p_l, axis=-1, keepdims=True)
        parts.append((p_c, p_l if with_lat else None, den))
    (p1c, p1l, den1), (p2c, p2l, den2) = parts
    r = lam * den1 / den2
    o = jnp.dot((p1c - r * p2c).astype(BF16), vc_ref[...], preferred_element_type=F32)
    if with_lat:
        o = o + jnp.dot((p1l - r * p2l).astype(BF16), v_ref[...], preferred_element_type=F32)
    o = o / den1
    o = o * lax.rsqrt(jnp.mean(o * o, axis=-1, keepdims=True) + EPS) * g_ref[0] * (1.0 - lambda_init)
    o_ref[...] = o.astype(BF16)


def _attention(qkv, lam_p, subln_g, mix_idx, lambda_init, ctx_out):
    t_out = T_ALL if ctx_out else T_LAT
    nqt = S // TQ
    kcol, vcol = D // VD, 2 * D // VD
    ctx_blk = T_LAT // L
    common = [
        pl.BlockSpec((1, 4, HD), lambda b, h, t: (mix_idx, 0, 0)),
        pl.BlockSpec((1, 1, VD), lambda b, h, t: (mix_idx, 0, 0)),
    ]
    ctx_kv = [
        pl.BlockSpec((L, VD), lambda b, h, t: (ctx_blk + b, kcol + h)),
        pl.BlockSpec((L, VD), lambda b, h, t: (ctx_blk + b, vcol + h)),
    ]
    g3 = subln_g.reshape(-1, 1, VD)
    o = pl.pallas_call(
        functools.partial(_attn_body, lambda_init=lambda_init, with_lat=True),
        grid=(B, H, nqt),
        in_specs=common + [pl.BlockSpec((TQ, VD), lambda b, h, t: (b * nqt + t, h))] + ctx_kv + [
            pl.BlockSpec((S, VD), lambda b, h, t: (b, kcol + h)),
            pl.BlockSpec((S, VD), lambda b, h, t: (b, vcol + h)),
        ],
        out_specs=pl.BlockSpec((TQ, VD), lambda b, h, t: (b * nqt + t, h)),
        out_shape=jax.ShapeDtypeStruct((t_out, D), BF16),
        compiler_params=_params("arbitrary", "arbitrary", "arbitrary"),
        name="attn_lat",
    )(lam_p, g3, qkv, qkv, qkv, qkv, qkv)
    if not ctx_out:
        return o
    return pl.pallas_call(
        functools.partial(_attn_body, lambda_init=lambda_init, with_lat=False),
        grid=(B, H, 1),
        in_specs=common + [pl.BlockSpec((L, VD), lambda b, h, t: (ctx_blk + b, h))] + ctx_kv + [
            pl.BlockSpec(memory_space=pl.ANY),
        ],
        out_specs=pl.BlockSpec((L, VD), lambda b, h, t: (ctx_blk + b, h)),
        out_shape=jax.ShapeDtypeStruct((t_out, D), BF16),
        input_output_aliases={5: 0},
        compiler_params=_params("arbitrary", "arbitrary", "arbitrary"),
        name="attn_ctx",
    )(lam_p, g3, qkv, qkv, qkv, o)


def _resid_body(x_ref, w_ref, *rest, has_bias):
    if has_bias:
        b_ref, r_ref, gate_ref, o_ref = rest
    else:
        r_ref, gate_ref, o_ref = rest
    x = x_ref[...]
    for c in range(D // MXU_COLS):
        cs = slice(c * MXU_COLS, (c + 1) * MXU_COLS)
        y = jnp.dot(x, w_ref[:, cs], preferred_element_type=F32)
        if has_bias:
            y = y + b_ref[0, :, cs]
        o_ref[:, cs] = r_ref[:, cs] + gate_ref[0, :, cs] * y


def _resid_proj(x, w, bias, resid, mods, layer, which, t_out, name):
    k = x.shape[1]
    tm = TM_RESID
    has_bias = bias is not None
    in_specs = [pl.BlockSpec((tm, k), lambda i: (i, 0)),
                pl.BlockSpec((k, D), lambda i: (0, 0), pipeline_mode=pl.Buffered(1))]
    args = [x, w]
    if has_bias:
        idx, arr = bias
        in_specs.append(_layer_vec_spec(idx))
        args.append(arr)
    in_specs += [pl.BlockSpec((tm, D), lambda i: (i, 0)), _mod_spec(layer, which, tm)]
    args += [resid, mods]
    return pl.pallas_call(
        functools.partial(_resid_body, has_bias=has_bias),
        grid=(t_out // tm,),
        in_specs=in_specs,
        out_specs=pl.BlockSpec((tm, D), lambda i: (i, 0)),
        out_shape=jax.ShapeDtypeStruct((t_out, D), F32),
        compiler_params=_params("arbitrary"),
        name=name,
    )(*args)


def _glu_body(x_ref, g_ref, sh_ref, sc_ref, wa_ref, wb_ref, *rest, kind):
    if kind == "glu":
        ba_ref, bb_ref, o_ref, hm_ref = rest
    else:
        o_ref, hm_ref = rest

    @pl.when(pl.program_id(1) == 0)
    def _():
        hm_ref[...] = _modulate(x_ref[...], g_ref[0], sh_ref[0], sc_ref[0]).astype(BF16)

    hm = hm_ref[...]
    for c in range(BN // MXU_COLS):
        cs = slice(c * MXU_COLS, (c + 1) * MXU_COLS)
        a = jnp.dot(hm, wa_ref[:, cs], preferred_element_type=F32)
        b = jnp.dot(hm, wb_ref[:, cs], preferred_element_type=F32)
        if kind == "glu":
            o = (a + ba_ref[0, :, cs]) * _sigmoid(b + bb_ref[0, :, cs])
        else:
            o = (a * _sigmoid(a)) * b
        o_ref[:, cs] = o.astype(o_ref.dtype)


def _glu_proj(h, mods, g, layer, w, bias, t_rows, kind, out_dtype, name):
    nh = w.shape[1] // 2
    nj = nh // BN
    in_specs = [
        pl.BlockSpec((TM, D), lambda i, j: (i, 0)),
        _layer_vec_spec(layer),
        _mod_spec(layer, 3 if kind == "swiglu" else 0, TM),
        _mod_spec(layer, 4 if kind == "swiglu" else 1, TM),
        pl.BlockSpec((D, BN), lambda i, j: (0, j)),
        pl.BlockSpec((D, BN), lambda i, j: (0, nj + j)),
    ]
    args = [h, g, mods, mods, w, w]
    if kind == "glu":
        idx, arr = bias
        in_specs += [pl.BlockSpec((1, 1, BN), lambda i, j: (idx, 0, j)),
                     pl.BlockSpec((1, 1, BN), lambda i, j: (idx, 0, nj + j))]
        args += [arr, arr]
    return pl.pallas_call(
        functools.partial(_glu_body, kind=kind),
        grid=(t_rows // TM, nj),
        in_specs=in_specs,
        out_specs=pl.BlockSpec((TM, BN), lambda i, j: (i, j)),
        out_shape=jax.ShapeDtypeStruct((t_rows, nh), out_dtype),
        scratch_shapes=[pltpu.VMEM((TM, D), BF16)],
        compiler_params=_params("arbitrary", "arbitrary"),
        name=name,
    )(*args)


def _conv_body(prev_ref, cur_ref, next_ref, w_ref, b_ref, lg_ref, lb_ref, o_ref, ext_ref, y_ref, sh_ref):
    i = pl.program_id(0)
    tiles_per_seq = S // TS
    is_ctx = i * TS >= T_LAT
    first = jnp.logical_or(is_ctx, i % tiles_per_seq == 0)
    last = jnp.logical_or(is_ctx, i % tiles_per_seq == tiles_per_seq - 1)
    ext_ref[0:HALO, :] = jnp.where(first, 0.0, prev_ref[...])
    ext_ref[HALO:HALO + TS, :] = cur_ref[...]
    ext_ref[HALO + TS:2 * HALO + TS, :] = jnp.where(last, 0.0, next_ref[...])

    rc = 64
    off = HALO - CPAD

    def chan(c, carry):
        cs = pl.ds(pl.multiple_of(c * LANES, LANES), LANES)
        for d in range(1, SUBLANES):
            sh_ref[d - 1] = ext_ref[pl.ds(d, TS + SH_EXTRA), cs]
        bias = b_ref[0, :, cs]
        accs = [jnp.broadcast_to(bias, (rc, LANES)) for _ in range(TS // rc)]
        for k in range(CW):
            q, d = divmod(k + off, SUBLANES)
            wk = jnp.broadcast_to(w_ref[0, k:k + 1, cs], (rc, LANES))
            for r in range(TS // rc):
                rows = pl.ds(r * rc + q * SUBLANES, rc)
                tap = ext_ref[rows, cs] if d == 0 else sh_ref[d - 1, rows, :]
                accs[r] = accs[r] + wk * tap
        for r in range(TS // rc):
            y_ref[pl.ds(r * rc, rc), cs] = accs[r]
        return carry

    lax.fori_loop(0, D // LANES, chan, 0)

    y = y_ref[...]
    yc = y - jnp.mean(y, axis=-1, keepdims=True)
    var = jnp.mean(yc * yc, axis=-1, keepdims=True)
    z = yc * lax.rsqrt(var + EPS) * lg_ref[0] + lb_ref[0]
    o_ref[...] = (z * _sigmoid(z)).astype(BF16)


def _conv_module(u, w_dw, b_dw, ln_g, ln_b, idx, t_rows):
    assert L == TS and TS % HALO == 0 and HALO >= CPAD
    hb = TS // HALO
    n_halo_blocks = u.shape[0] // HALO
    return pl.pallas_call(
        _conv_body,
        grid=(t_rows // TS,),
        in_specs=[
            pl.BlockSpec((HALO, D), lambda i: (jnp.maximum(i * hb - 1, 0), 0)),
            pl.BlockSpec((TS, D), lambda i: (i, 0)),
            pl.BlockSpec((HALO, D), lambda i: (jnp.minimum((i + 1) * hb, n_halo_blocks - 1), 0)),
            pl.BlockSpec((1, CW, D), lambda i: (idx, 0, 0)),
            _layer_vec_spec(idx),
            _layer_vec_spec(idx),
            _layer_vec_spec(idx),
        ],
        out_specs=pl.BlockSpec((TS, D), lambda i: (i, 0)),
        out_shape=jax.ShapeDtypeStruct((t_rows, D), BF16),
        scratch_shapes=[pltpu.VMEM((TS + 2 * HALO, D), F32), pltpu.VMEM((TS, D), F32),
                        pltpu.VMEM((SUBLANES - 1, TS + SH_EXTRA, LANES), F32)],
        compiler_params=_params("arbitrary"),
        name="dwconv_ln_silu",
    )(u, u, u, w_dw, b_dw, ln_g, ln_b)


def _router_body(x_ref, g_ref, sh_ref, sc_ref, wr_ref, hm_ref, info_ref):
    hm = _modulate(x_ref[...], g_ref[0], sh_ref[0], sc_ref[0])
    for s in range(SLABS):
        hm_ref[pl.ds(s, ROUTER_TM, stride=SLABS), :] = hm[:, s * LANES:(s + 1) * LANES]
    logits = jnp.dot(hm, wr_ref[...], precision=lax.Precision.HIGHEST, preferred_element_type=F32)
    lane = lax.broadcasted_iota(jnp.int32, logits.shape, 1).astype(F32)
    ninf = -jnp.inf
    lg = jnp.where(lane < E, logits, ninf)
    v1 = jnp.max(lg, axis=-1, keepdims=True)
    i1 = jnp.min(jnp.where(lg == v1, lane, float(LANES)), axis=-1, keepdims=True)
    lg2 = jnp.where(lane == i1, ninf, lg)
    v2 = jnp.max(lg2, axis=-1, keepdims=True)
    i2 = jnp.min(jnp.where(lg2 == v2, lane, float(LANES)), axis=-1, keepdims=True)
    e2 = jnp.exp(v2 - v1)
    w1 = 1.0 / (1.0 + e2)
    w2 = e2 * w1
    info_ref[...] = jnp.where(lane == 0, i1, jnp.where(lane == 1, i2,
                              jnp.where(lane == 2, w1, jnp.where(lane == 3, w2, 0.0))))


def _router(h, mods, g, layer, w_router_pad, t_rows):
    tm = ROUTER_TM
    return pl.pallas_call(
        _router_body,
        grid=(t_rows // tm,),
        in_specs=[
            pl.BlockSpec((tm, D), lambda i: (i, 0)),
            _layer_vec_spec(layer),
            _mod_spec(layer, 3, tm),
            _mod_spec(layer, 4, tm),
            pl.BlockSpec((D, LANES), lambda i: (0, 0)),
        ],
        out_specs=[pl.BlockSpec((tm * SLABS, LANES), lambda i: (i, 0)),
                   pl.BlockSpec((tm, LANES), lambda i: (i, 0))],
        out_shape=[jax.ShapeDtypeStruct((t_rows * SLABS, LANES), F32),
                   jax.ShapeDtypeStruct((t_rows, LANES), F32)],
        compiler_params=_params("arbitrary"),
        name="router",
    )(h, g, mods, mods, w_router_pad)


def _row_gather_step(i, n_steps, idx_ref, idx_next_ref, src_ref, buf, sem, rows, consume):
    def row_copy(src_row, r, slot):
        return pltpu.make_async_copy(
            src_ref.at[pl.ds(pl.multiple_of(src_row * SLABS, SLABS), SLABS)],
            buf.at[slot, pl.ds(pl.multiple_of(r * SLABS, SLABS), SLABS)],
            sem.at[slot])

    def issue(ref, slot):
        def body(r, carry):
            row_copy(ref[0, 0, r], r, slot).start()
            return carry
        lax.fori_loop(0, rows, body, 0, unroll=8)

    def step(slot):
        if slot == 0:
            @pl.when(i == 0)
            def _():
                issue(idx_ref, 0)

        @pl.when(i + 1 < n_steps)
        def _():
            issue(idx_next_ref, 1 - slot)

        pltpu.make_async_copy(src_ref.at[pl.ds(0, rows * SLABS)], buf.at[slot], sem.at[slot]).wait()
        consume(slot)

    @pl.when(i % 2 == 0)
    def _():
        step(0)

    @pl.when(i % 2 == 1)
    def _():
        step(1)


def _gather_body(idx_ref, idx_next_ref, src_ref, o_ref, buf, sem):
    def consume(slot):
        for s in range(SLABS):
            o_ref[:, s * LANES:(s + 1) * LANES] = buf[slot, pl.ds(s, GATHER_ROWS, stride=SLABS), :].astype(BF16)

    _row_gather_step(pl.program_id(0), pl.num_programs(0), idx_ref, idx_next_ref, src_ref, buf, sem,
                     GATHER_ROWS, consume)


def _gather_rows(src_slabs, idx):
    n = idx.shape[0]
    assert n % GATHER_ROWS == 0
    nsteps = n // GATHER_ROWS
    idx3 = idx.reshape(nsteps, 1, GATHER_ROWS)
    return pl.pallas_call(
        _gather_body,
        grid=(nsteps,),
        in_specs=[pl.BlockSpec((1, 1, GATHER_ROWS), lambda i: (i, 0, 0), memory_space=pltpu.SMEM),
                  pl.BlockSpec((1, 1, GATHER_ROWS), lambda i: (jnp.minimum(i + 1, nsteps - 1), 0, 0),
                               memory_space=pltpu.SMEM),
                  pl.BlockSpec(memory_space=pl.ANY)],
        out_specs=pl.BlockSpec((GATHER_ROWS, D), lambda i: (i, 0)),
        out_shape=jax.ShapeDtypeStruct((n, D), BF16),
        scratch_shapes=[pltpu.VMEM((2, GATHER_ROWS * SLABS, LANES), F32), pltpu.SemaphoreType.DMA((2,))],
        compiler_params=_params("arbitrary"),
        name="moe_gather",
    )(idx3, idx3, src_slabs)


def _moe_up_body(te_ref, nu_ref, x_ref, wg_ref, wu_ref, o_ref, wg_bf, wu_bf):
    i = pl.program_id(1)

    @pl.when(i < nu_ref[0])
    def _():
        @pl.when(jnp.logical_or(i == 0, te_ref[i] != te_ref[jnp.maximum(i - 1, 0)]))
        def _():
            wg_bf[...] = wg_ref[0, 0].astype(BF16)
            wu_bf[...] = wu_ref[0, 0].astype(BF16)

        x = x_ref[...]
        for c in range(BN_UP // MXU_COLS):
            cs = slice(c * MXU_COLS, (c + 1) * MXU_COLS)
            a = jnp.dot(x, wg_bf[:, cs], preferred_element_type=F32)
            b = jnp.dot(x, wu_bf[:, cs], preferred_element_type=F32)
            o_ref[:, cs] = ((a * _sigmoid(a)) * b).astype(BF16)


def _moe_down_body(te_ref, nu_ref, a_ref, wd_ref, o_ref):
    @pl.when(pl.program_id(0) < nu_ref[0])
    def _():
        a = a_ref[...]
        for c in range(D // MXU_COLS):
            y = jnp.dot(a, wd_ref[0, :, c * MXU_COLS:(c + 1) * MXU_COLS], preferred_element_type=F32)
            for k in range(MXU_COLS // LANES):
                s = c * (MXU_COLS // LANES) + k
                o_ref[pl.ds(s, TME_DOWN, stride=SLABS), :] = y[:, k * LANES:(k + 1) * LANES]


def _moe_experts(xs, w_gu_all, moe_idx, w_down, tile_expert, n_used):
    n_tiles = xs.shape[0] // TME_UP
    nj = DFF // BN_UP

    def tile(i, nu):
        return jnp.minimum(i, nu[0] - 1)

    act = pl.pallas_call(
        _moe_up_body,
        grid_spec=pltpu.PrefetchScalarGridSpec(
            num_scalar_prefetch=2,
            grid=(nj, n_tiles),
            in_specs=[
                pl.BlockSpec((TME_UP, D), lambda j, i, te, nu: (tile(i, nu), 0)),
                pl.BlockSpec((1, 1, D, BN_UP), lambda j, i, te, nu: (moe_idx, te[tile(i, nu)], 0, j)),
                pl.BlockSpec((1, 1, D, BN_UP), lambda j, i, te, nu: (moe_idx, te[tile(i, nu)], 0, nj + j)),
            ],
            out_specs=pl.BlockSpec((TME_UP, BN_UP), lambda j, i, te, nu: (tile(i, nu), j)),
            scratch_shapes=[pltpu.VMEM((D, BN_UP), BF16), pltpu.VMEM((D, BN_UP), BF16)],
        ),
        out_shape=jax.ShapeDtypeStruct((n_tiles * TME_UP, DFF), BF16),
        compiler_params=_params("arbitrary", "arbitrary"),
        name="moe_up",
    )(tile_expert, n_used, xs, w_gu_all, w_gu_all)

    split = TME_UP // TME_DOWN
    return pl.pallas_call(
        _moe_down_body,
        grid_spec=pltpu.PrefetchScalarGridSpec(
            num_scalar_prefetch=2,
            grid=(n_tiles * split,),
            in_specs=[
                pl.BlockSpec((TME_DOWN, DFF), lambda i, te, nu: (tile(i, nu), 0)),
                pl.BlockSpec((1, DFF, D), lambda i, te, nu: (te[tile(i, nu)], 0, 0),
                             pipeline_mode=pl.Buffered(1)),
            ],
            out_specs=pl.BlockSpec((TME_DOWN * SLABS, LANES), lambda i, te, nu: (tile(i, nu), 0)),
        ),
        out_shape=jax.ShapeDtypeStruct((n_tiles * TME_UP * SLABS, LANES), F32),
        compiler_params=_params("arbitrary"),
        name="moe_down",
    )(jnp.repeat(tile_expert, split), n_used * split, act, w_down)


def _routing_tables(info, t_rows, n_tiles):
    idx = info[:, :2].astype(jnp.int32)
    flat_e = idx.T.reshape(-1)
    onehot = (flat_e[:, None] == jnp.arange(E, dtype=jnp.int32)[None, :]).astype(jnp.int32)
    csum = jnp.cumsum(onehot, axis=0)
    rank = jnp.sum((csum - onehot) * onehot, axis=1)
    counts = csum[-1]
    padded = ((counts + TME - 1) // TME) * TME
    ends = jnp.cumsum(padded)
    starts = ends - padded
    pos = starts[flat_e] + rank
    n_rows = n_tiles * TME
    tile_start = jnp.arange(n_tiles, dtype=jnp.int32) * TME
    tile_expert = jnp.minimum(jnp.sum((tile_start[:, None] >= ends[None, :]).astype(jnp.int32), axis=1), E - 1)
    n_used = (ends[-1] // TME).astype(jnp.int32).reshape(1)
    token_of_pos = jnp.zeros((n_rows,), jnp.int32).at[pos].set(
        jnp.arange(2 * t_rows, dtype=jnp.int32) % t_rows)
    return pos.astype(jnp.int32), token_of_pos, tile_expert.astype(jnp.int32), n_used


def _combine_body(idx_ref, idx_next_ref, ys_ref, h_ref, info_ref, gate_ref, *rest, final):
    if final:
        fg_ref, o_ref, buf, sem = rest
    else:
        o_ref, buf, sem = rest

    def consume(slot):
        w0 = jnp.broadcast_to(info_ref[:, 2:3], (COMBINE_TM, LANES))
        w1 = jnp.broadcast_to(info_ref[:, 3:4], (COMBINE_TM, LANES))
        for s in range(SLABS):
            cols = slice(s * LANES, (s + 1) * LANES)
            y = (w0 * buf[slot, pl.ds(s, COMBINE_TM, stride=SLABS), :]
                 + w1 * buf[slot, pl.ds(COMBINE_TM * SLABS + s, COMBINE_TM, stride=SLABS), :])
            o_ref[:, cols] = h_ref[:, cols] + gate_ref[0, :, cols] * y
        if final:
            h = o_ref[...]
            o_ref[...] = (h * lax.rsqrt(jnp.mean(h * h, axis=-1, keepdims=True) + EPS)) * fg_ref[...]

    _row_gather_step(pl.program_id(0), pl.num_programs(0), idx_ref, idx_next_ref, ys_ref, buf, sem,
                     2 * COMBINE_TM, consume)


def _combine(h, ys_slabs, pos, info, mods, layer, t_rows, final_g):
    tm = COMBINE_TM
    nt = t_rows // tm
    final = final_g is not None
    idx3 = pos.reshape(2, nt, tm).transpose(1, 0, 2).reshape(nt, 1, 2 * tm)
    in_specs = [pl.BlockSpec((1, 1, 2 * tm), lambda i: (i, 0, 0), memory_space=pltpu.SMEM),
                pl.BlockSpec((1, 1, 2 * tm), lambda i: (jnp.minimum(i + 1, nt - 1), 0, 0),
                             memory_space=pltpu.SMEM),
                pl.BlockSpec(memory_space=pl.ANY),
                pl.BlockSpec((tm, D), lambda i: (i, 0)),
                pl.BlockSpec((tm, LANES), lambda i: (i, 0)),
                _mod_spec(layer, 5, tm)]
    args = [idx3, idx3, ys_slabs, h, info, mods]
    if final:
        in_specs.append(pl.BlockSpec((1, D), lambda i: (0, 0)))
        args.append(final_g.reshape(1, D))
    return pl.pallas_call(
        functools.partial(_combine_body, final=final),
        grid=(nt,),
        in_specs=in_specs,
        out_specs=pl.BlockSpec((tm, D), lambda i: (i, 0)),
        out_shape=jax.ShapeDtypeStruct((t_rows, D), F32),
        scratch_shapes=[pltpu.VMEM((2, 2 * tm * SLABS, LANES), F32), pltpu.SemaphoreType.DMA((2,))],
        compiler_params=_params("arbitrary"),
        name="moe_combine",
    )(*args)


def _rope_tables():
    rows = S // GRID_W
    row = jnp.repeat(jnp.arange(rows, dtype=F32), GRID_W)
    col = jnp.tile(jnp.arange(GRID_W, dtype=F32), rows)
    n_freq = HD // 4
    inv_freq = ROPE_BASE ** (-jnp.arange(n_freq, dtype=F32) / n_freq)
    ang = jnp.concatenate([row[:, None] * inv_freq, col[:, None] * inv_freq], axis=-1)
    ang = jnp.concatenate([ang, ang], axis=-1)
    sign = jnp.where(jnp.arange(HD) < HD // 2, -1.0, 1.0).astype(F32)
    cos = jnp.concatenate([jnp.cos(ang), jnp.ones((TM, HD), F32)], axis=0)
    sin = jnp.concatenate([jnp.sin(ang) * sign, jnp.zeros((TM, HD), F32)], axis=0)
    scale = HD ** -0.5 * math.log2(math.e)
    return jnp.stack([cos * scale, cos]), jnp.stack([sin * scale, sin])


def kernel(x, c, ctx, c_ctx, ada_w, ada_b, norm_mix_g, norm_ffn_g, attn_w_qkv, attn_lambda, attn_subln_g, attn_w_o, conv_w_in, conv_b_in, conv_w_dw, conv_b_dw, conv_ln_g, conv_ln_b, conv_w_out, conv_b_out, ffn_w_gu, ffn_w_down, moe_router, moe_w_gu, moe_w_down, final_g):
    h = jnp.concatenate([x.reshape(T_LAT, D), ctx.reshape(T_CTX, D)], axis=0)
    cond = jnp.concatenate([c, c_ctx[None, :], jnp.zeros((COND_ROWS - B - 1, D), F32)], axis=0)
    mods = _ada_mods(cond, ada_w, ada_b)
    cos_t, sin_t = _rope_tables()
    mix_g = norm_mix_g.reshape(DEPTH, 1, D)
    ffn_g = norm_ffn_g.reshape(DEPTH, 1, D)

    for i in range(DEPTH):
        mix_idx = i // 2
        ffn_idx = i // 2
        ctx_live = i < LAST_CTX_READER
        t_rows = T_ALL if ctx_live else T_LAT
        if i % 2 == 0:
            lambda_init = 0.8 - 0.6 * math.exp(-0.3 * i)
            qkv = _qkv_proj(h, mods, mix_g, i, attn_w_qkv[mix_idx].astype(BF16), cos_t, sin_t)
            o = _attention(qkv, attn_lambda, attn_subln_g, mix_idx, lambda_init, ctx_live)
            h = _resid_proj(o, attn_w_o[mix_idx].astype(BF16), None, h, mods, i, 2, t_rows, "attn_out")
        else:
            u = _glu_proj(h, mods, mix_g, i, conv_w_in[mix_idx].astype(BF16),
                          (mix_idx, conv_b_in.reshape(-1, 1, 2 * D)), t_rows, "glu", F32, "conv_in")
            z = _conv_module(u, conv_w_dw, conv_b_dw.reshape(-1, 1, D), conv_ln_g.reshape(-1, 1, D),
                             conv_ln_b.reshape(-1, 1, D), mix_idx, t_rows)
            h = _resid_proj(z, conv_w_out[mix_idx].astype(BF16), (mix_idx, conv_b_out.reshape(-1, 1, D)),
                            h, mods, i, 2, t_rows, "conv_out")
        if i % 2 == 0:
            act = _glu_proj(h, mods, ffn_g, i, ffn_w_gu[ffn_idx].astype(BF16), None, t_rows,
                            "swiglu", BF16, "ffn_up")
            h = _resid_proj(act, ffn_w_down[ffn_idx].astype(BF16), None, h, mods, i, 5, t_rows, "ffn_down")
        else:
            n_tiles = N_TILES_E if ctx_live else N_TILES_E_LAT
            w_r = jnp.zeros((D, LANES), F32).at[:, :E].set(moe_router[ffn_idx])
            hm_slabs, info = _router(h, mods, ffn_g, i, w_r, t_rows)
            pos, token_of_pos, tile_expert, n_used = _routing_tables(info, t_rows, n_tiles)
            xs = _gather_rows(hm_slabs, token_of_pos)
            ys_slabs = _moe_experts(xs, moe_w_gu, ffn_idx, moe_w_down[ffn_idx].astype(BF16),
                                    tile_expert, n_used)
            h = _combine(h, ys_slabs, pos, info, mods, i, t_rows, final_g if i == DEPTH - 1 else None)
    return h.reshape(B, S, D)
```

```python
import functools
import math

import jax
import jax.numpy as jnp
from jax import lax
from jax.experimental import pallas as pl
from jax.experimental.pallas import tpu as pltpu

F32 = jnp.float32
BF16 = jnp.bfloat16

D = 2048
B = 16
S = 2048
L = 256
DEPTH = 4
GRID_W = 64
H = 8
HD = 128
VD = 2 * HD
ROPE_BASE = 10000.0
CW = 31
CPAD = (CW - 1) // 2
DFF = 5632
E = 8
NMOD = 6
EPS = 1e-6

T_LAT = B * S
T_CTX = B * L
T_ALL = T_LAT + T_CTX
COND_ROWS = 24
CTX_COND_ROW = B
LAST_CTX_READER = 2

V7X_VMEM_LIMIT_BYTES = 56 * 1024 * 1024
LANES = 128
SUBLANES = 8
HALO = 16
SH_EXTRA = SUBLANES * ((CW - 1 + HALO - CPAD) // SUBLANES)

TM = 1024
BN = 512
BN_QKV = 1024
TM_RESID = 512
TQ = 1024
TQ_SUB = 128
TS = 256
MXU_COLS = 256
TME = 1024
TME_UP = TME
TME_DOWN = 512
BN_UP = 512
SLABS = D // LANES
ROW_PITCH = SLABS + SUBLANES
ROUTER_TM = 512
GATHER_ROWS = 256
COMBINE_TM = 256
N_TILES_E = (2 * T_ALL + E * (TME - 1) + TME - 1) // TME
N_TILES_E_LAT = (2 * T_LAT + E * (TME - 1) + TME - 1) // TME


def _params(*sem):
    return pltpu.CompilerParams(dimension_semantics=sem, vmem_limit_bytes=V7X_VMEM_LIMIT_BYTES)


def _sigmoid(x):
    return 1.0 / (1.0 + jnp.exp(-x))


def _modulate(x, g, shift, scale):
    xn = x * lax.rsqrt(jnp.mean(x * x, axis=-1, keepdims=True) + EPS)
    return (xn * g) * (1.0 + scale) + shift


def _modulate_into(x_ref, g_ref, sh_ref, sc_ref, out_ref):
    g, shift, scale = g_ref[0], sh_ref[0], sc_ref[0]
    rb = 2 * SUBLANES

    def body(r, carry):
        rows = pl.ds(pl.multiple_of(r * rb, rb), rb)
        out_ref[rows, :] = _modulate(x_ref[rows, :], g, shift, scale).astype(out_ref.dtype)
        return carry

    lax.fori_loop(0, x_ref.shape[0] // rb, body, 0, unroll=8)


def _cond_row(i, tm):
    return jnp.where(i * tm < T_LAT, (i * tm) // S, CTX_COND_ROW)


def _mod_spec(layer, which, tm):
    def imap(i, *rest):
        return ((layer * COND_ROWS + _cond_row(i, tm)) * NMOD + which, 0, 0)
    return pl.BlockSpec((1, 1, D), imap)


def _layer_vec_spec(idx):
    return pl.BlockSpec((1, 1, D), lambda *grid: (idx, 0, 0))


def _ada_body(c_ref, w_ref, b_ref, o_ref):
    x = c_ref[...]
    s = (x * _sigmoid(x)).astype(BF16)
    o_ref[0] = jnp.dot(s, w_ref[0].astype(BF16), preferred_element_type=F32) + b_ref[0]


def _ada_mods(cond, ada_w, ada_b):
    bn = 1024
    n = NMOD * D
    out = pl.pallas_call(
        _ada_body,
        grid=(DEPTH, n // bn),
        in_specs=[
            pl.BlockSpec((COND_ROWS, D), lambda l, j: (0, 0)),
            pl.BlockSpec((1, D, bn), lambda l, j: (l, 0, j)),
            pl.BlockSpec((1, 1, bn), lambda l, j: (l, 0, j)),
        ],
        out_specs=pl.BlockSpec((1, COND_ROWS, bn), lambda l, j: (l, 0, j)),
        out_shape=jax.ShapeDtypeStruct((DEPTH, COND_ROWS, n), F32),
        compiler_params=_params("arbitrary", "arbitrary"),
        name="ada_mods",
    )(cond, ada_w, ada_b.reshape(DEPTH, 1, n))
    return out.reshape(DEPTH * COND_ROWS * NMOD, 1, D)


def _qkv_body(x_ref, g_ref, sh_ref, sc_ref, w_ref, cos_ref, sin_ref, o_ref, hm_ref, *, n_rope):
    j = pl.program_id(1)

    @pl.when(j == 0)
    def _():
        _modulate_into(x_ref, g_ref, sh_ref, sc_ref, hm_ref)

    hm = hm_ref[...]

    def chunk(c):
        return jnp.dot(hm, w_ref[:, c * MXU_COLS:(c + 1) * MXU_COLS], preferred_element_type=F32)

    @pl.when(j < n_rope)
    def _():
        cos = cos_ref[0]
        sin = sin_ref[0]
        for c in range(BN_QKV // MXU_COLS):
            acc = chunk(c)
            for k in range(MXU_COLS // HD):
                xc = acc[:, k * HD:(k + 1) * HD]
                col = c * MXU_COLS + k * HD
                o_ref[:, col:col + HD] = (xc * cos + pltpu.roll(xc, HD // 2, 1) * sin).astype(BF16)

    @pl.when(j >= n_rope)
    def _():
        for c in range(BN_QKV // MXU_COLS):
            o_ref[:, c * MXU_COLS:(c + 1) * MXU_COLS] = chunk(c).astype(BF16)


def _qkv_proj(h, mods, g, layer, w_qkv, cos_t, sin_t):
    nt = T_ALL // TM
    nq = D // BN_QKV
    tiles_per_seq = S // TM

    def rope_map(i, j):
        return (jnp.where(j < nq, 0, 1),
                jnp.where(i * TM < T_LAT, i % tiles_per_seq, tiles_per_seq), 0)

    return pl.pallas_call(
        functools.partial(_qkv_body, n_rope=2 * nq),
        grid=(nt, 3 * D // BN_QKV),
        in_specs=[
            pl.BlockSpec((TM, D), lambda i, j: (i, 0)),
            _layer_vec_spec(layer),
            _mod_spec(layer, 0, TM),
            _mod_spec(layer, 1, TM),
            pl.BlockSpec((D, BN_QKV), lambda i, j: (0, j)),
            pl.BlockSpec((1, TM, HD), rope_map),
            pl.BlockSpec((1, TM, HD), rope_map),
        ],
        out_specs=pl.BlockSpec((TM, BN_QKV), lambda i, j: (i, j)),
        out_shape=jax.ShapeDtypeStruct((T_ALL, 3 * D), BF16),
        scratch_shapes=[pltpu.VMEM((TM, D), BF16)],
        compiler_params=_params("arbitrary", "arbitrary"),
        name="qkv_proj",
    )(h, g, mods, mods, w_qkv, cos_t, sin_t)


def _attn_body(lam_ref, g_ref, q_ref, kc_ref, vc_ref, *rest, lambda_init, with_lat):
    if with_lat:
        k_ref, v_ref, o_ref = rest
    else:
        o_ref = rest[-1]
    lp = lam_ref[0]
    lam = (jnp.exp(jnp.sum(lp[0:1] * lp[1:2], axis=-1, keepdims=True))
           - jnp.exp(jnp.sum(lp[2:3] * lp[3:4], axis=-1, keepdims=True)) + lambda_init)
    nt_dims = (((1,), (1,)), ((), ()))
    n_rows = q_ref.shape[0]
    sub = min(n_rows, TQ_SUB)
    def scores(t):
        out = []
        for c in range(2):
            sl = slice(c * HD, (c + 1) * HD)
            qc = q_ref[t * sub:(t + 1) * sub, sl]
            s_c = lax.dot_general(qc, kc_ref[:, sl], nt_dims, preferred_element_type=F32)
            s_l = lax.dot_general(qc, k_ref[:, sl], nt_dims, preferred_element_type=F32) if with_lat else None
            out.append((s_c, s_l))
        return out

    def weights(sc):
        parts = []
        for s_c, s_l in sc:
            m = jnp.max(s_c, axis=-1, keepdims=True)
            if with_lat:
                m = jnp.maximum(m, jnp.max(s_l, axis=-1, keepdims=True))
                p_l = jnp.exp2(s_l - m)
            p_c = jnp.exp2(s_c - m)
            den = jnp.sum(p_c, axis=-1, keepdims=True)
            if with_lat:
                den = den + jnp.sum(p_l, axis=-1, keepdims=True)
            parts.append((p_c, p_l if with_lat else None, den))
        (p1c, p1l, den1), (p2c, p2l, den2) = parts
        r = lam * den1 / den2
        a_c = (p1c - r * p2c).astype(BF16)
        a_l = (p1l - r * p2l).astype(BF16) if with_lat else None
        return a_c, a_l, den1

    def output(t, w):
        a_c, a_l, den1 = w
        o = jnp.dot(a_c, vc_ref[...], preferred_element_type=F32)
        if with_lat:
            o = o + jnp.dot(a_l, v_ref[...], preferred_element_type=F32)
        o = o / den1
        o = o * lax.rsqrt(jnp.mean(o * o, axis=-1, keepdims=True) + EPS) * g_ref[0] * (1.0 - lambda_init)
        o_ref[t * sub:(t + 1) * sub, :] = o.astype(BF16)

    n_sub = n_rows // sub
    nxt = scores(0)
    for t in range(n_sub):
        cur = nxt
        if t + 1 < n_sub:
            nxt = scores(t + 1)
        output(t, weights(cur))


def _attention(qkv, lam_p, subln_g, mix_idx, lambda_init, ctx_out):
    t_out = T_ALL if ctx_out else T_LAT
    nqt = S // TQ
    kcol, vcol = D // VD, 2 * D // VD
    ctx_blk = T_LAT // L
    common = [
        pl.BlockSpec((1, 4, HD), lambda b, h, t: (mix_idx, 0, 0)),
        pl.BlockSpec((1, 1, VD), lambda b, h, t: (mix_idx, 0, 0)),
    ]
    ctx_kv = [
        pl.BlockSpec((L, VD), lambda b, h, t: (ctx_blk + b, kcol + h)),
        pl.BlockSpec((L, VD), lambda b, h, t: (ctx_blk + b, vcol + h)),
    ]
    g3 = subln_g.reshape(-1, 1, VD)
    o = pl.pallas_call(
        functools.partial(_attn_body, lambda_init=lambda_init, with_lat=True),
        grid=(B, H, nqt),
        in_specs=common + [pl.BlockSpec((TQ, VD), lambda b, h, t: (b * nqt + t, h))] + ctx_kv + [
            pl.BlockSpec((S, VD), lambda b, h, t: (b, kcol + h)),
            pl.BlockSpec((S, VD), lambda b, h, t: (b, vcol + h)),
        ],
        out_specs=pl.BlockSpec((TQ, VD), lambda b, h, t: (b * nqt + t, h)),
        out_shape=jax.ShapeDtypeStruct((t_out, D), BF16),
        compiler_params=_params("arbitrary", "arbitrary", "arbitrary"),
        name="attn_lat",
    )(lam_p, g3, qkv, qkv, qkv, qkv, qkv)
    if not ctx_out:
        return o
    return pl.pallas_call(
        functools.partial(_attn_body, lambda_init=lambda_init, with_lat=False),
        grid=(B, H, 1),
        in_specs=common + [pl.BlockSpec((L, VD), lambda b, h, t: (ctx_blk + b, h))] + ctx_kv + [
            pl.BlockSpec(memory_space=pl.ANY),
        ],
        out_specs=pl.BlockSpec((L, VD), lambda b, h, t: (ctx_blk + b, h)),
        out_shape=jax.ShapeDtypeStruct((t_out, D), BF16),
        input_output_aliases={5: 0},
        compiler_params=_params("arbitrary", "arbitrary", "arbitrary"),
        name="attn_ctx",
    )(lam_p, g3, qkv, qkv, qkv, o)


def _resid_body(x_ref, w_ref, *rest, has_bias):
    if has_bias:
        b_ref, r_ref, gate_ref, o_ref = rest
    else:
        r_ref, gate_ref, o_ref = rest
    x = x_ref[...]
    for c in range(D // MXU_COLS):
        cs = slice(c * MXU_COLS, (c + 1) * MXU_COLS)
        y = jnp.dot(x, w_ref[:, cs], preferred_element_type=F32)
        if has_bias:
            y = y + b_ref[0, :, cs]
        o_ref[:, cs] = r_ref[:, cs] + gate_ref[0, :, cs] * y


def _resid_proj(x, w, bias, resid, mods, layer, which, t_out, name):
    k = x.shape[1]
    tm = TM_RESID
    has_bias = bias is not None
    in_specs = [pl.BlockSpec((tm, k), lambda i: (i, 0)),
                pl.BlockSpec((k, D), lambda i: (0, 0), pipeline_mode=pl.Buffered(1))]
    args = [x, w]
    if has_bias:
        idx, arr = bias
        in_specs.append(_layer_vec_spec(idx))
        args.append(arr)
    in_specs += [pl.BlockSpec((tm, D), lambda i: (i, 0)), _mod_spec(layer, which, tm)]
    args += [resid, mods]
    return pl.pallas_call(
        functools.partial(_resid_body, has_bias=has_bias),
        grid=(t_out // tm,),
        in_specs=in_specs,
        out_specs=pl.BlockSpec((tm, D), lambda i: (i, 0)),
        out_shape=jax.ShapeDtypeStruct((t_out, D), F32),
        compiler_params=_params("arbitrary"),
        name=name,
    )(*args)


def _glu_body(x_ref, g_ref, sh_ref, sc_ref, wa_ref, wb_ref, *rest, kind):
    if kind == "glu":
        ba_ref, bb_ref, o_ref, hm_ref = rest
    else:
        o_ref, hm_ref = rest

    @pl.when(pl.program_id(1) == 0)
    def _():
        _modulate_into(x_ref, g_ref, sh_ref, sc_ref, hm_ref)

    hm = hm_ref[...]
    for c in range(BN // MXU_COLS):
        cs = slice(c * MXU_COLS, (c + 1) * MXU_COLS)
        a = jnp.dot(hm, wa_ref[:, cs], preferred_element_type=F32)
        b = jnp.dot(hm, wb_ref[:, cs], preferred_element_type=F32)
        if kind == "glu":
            o = (a + ba_ref[0, :, cs]) * _sigmoid(b + bb_ref[0, :, cs])
        else:
            o = (a * _sigmoid(a)) * b
        o_ref[:, cs] = o.astype(o_ref.dtype)


def _glu_proj(h, mods, g, layer, w, bias, t_rows, kind, out_dtype, name):
    nh = w.shape[1] // 2
    nj = nh // BN
    in_specs = [
        pl.BlockSpec((TM, D), lambda i, j: (i, 0)),
        _layer_vec_spec(layer),
        _mod_spec(layer, 3 if kind == "swiglu" else 0, TM),
        _mod_spec(layer, 4 if kind == "swiglu" else 1, TM),
        pl.BlockSpec((D, BN), lambda i, j: (0, j)),
        pl.BlockSpec((D, BN), lambda i, j: (0, nj + j)),
    ]
    args = [h, g, mods, mods, w, w]
    if kind == "glu":
        idx, arr = bias
        in_specs += [pl.BlockSpec((1, 1, BN), lambda i, j: (idx, 0, j)),
                     pl.BlockSpec((1, 1, BN), lambda i, j: (idx, 0, nj + j))]
        args += [arr, arr]
    return pl.pallas_call(
        functools.partial(_glu_body, kind=kind),
        grid=(t_rows // TM, nj),
        in_specs=in_specs,
        out_specs=pl.BlockSpec((TM, BN), lambda i, j: (i, j)),
        out_shape=jax.ShapeDtypeStruct((t_rows, nh), out_dtype),
        scratch_shapes=[pltpu.VMEM((TM, D), BF16)],
        compiler_params=_params("arbitrary", "arbitrary"),
        name=name,
    )(*args)


def _conv_body(prev_ref, cur_ref, next_ref, w_ref, b_ref, lg_ref, lb_ref, o_ref, ext_ref, y_ref, sh_ref):
    i = pl.program_id(0)
    tiles_per_seq = S // TS
    is_ctx = i * TS >= T_LAT
    first = jnp.logical_or(is_ctx, i % tiles_per_seq == 0)
    last = jnp.logical_or(is_ctx, i % tiles_per_seq == tiles_per_seq - 1)
    ext_ref[0:HALO, :] = jnp.where(first, 0.0, prev_ref[...])
    ext_ref[HALO:HALO + TS, :] = cur_ref[...]
    ext_ref[HALO + TS:2 * HALO + TS, :] = jnp.where(last, 0.0, next_ref[...])

    rc = 64
    off = HALO - CPAD

    def chan(c, carry):
        cs = pl.ds(pl.multiple_of(c * LANES, LANES), LANES)
        for d in range(1, SUBLANES):
            sh_ref[d - 1] = ext_ref[pl.ds(d, TS + SH_EXTRA), cs]
        bias = b_ref[0, :, cs]
        accs = [jnp.broadcast_to(bias, (rc, LANES)) for _ in range(TS // rc)]
        for k in range(CW):
            q, d = divmod(k + off, SUBLANES)
            wk = jnp.broadcast_to(w_ref[0, k:k + 1, cs], (rc, LANES))
            for r in range(TS // rc):
                rows = pl.ds(r * rc + q * SUBLANES, rc)
                tap = ext_ref[rows, cs] if d == 0 else sh_ref[d - 1, rows, :]
                accs[r] = accs[r] + wk * tap
        for r in range(TS // rc):
            y_ref[pl.ds(r * rc, rc), cs] = accs[r]
        return carry

    lax.fori_loop(0, D // LANES, chan, 0)

    y = y_ref[...]
    yc = y - jnp.mean(y, axis=-1, keepdims=True)
    var = jnp.mean(yc * yc, axis=-1, keepdims=True)
    z = yc * lax.rsqrt(var + EPS) * lg_ref[0] + lb_ref[0]
    o_ref[...] = (z * _sigmoid(z)).astype(BF16)


def _conv_module(u, w_dw, b_dw, ln_g, ln_b, idx, t_rows):
    assert L == TS and TS % HALO == 0 and HALO >= CPAD
    hb = TS // HALO
    n_halo_blocks = u.shape[0] // HALO
    return pl.pallas_call(
        _conv_body,
        grid=(t_rows // TS,),
        in_specs=[
            pl.BlockSpec((HALO, D), lambda i: (jnp.maximum(i * hb - 1, 0), 0)),
            pl.BlockSpec((TS, D), lambda i: (i, 0)),
            pl.BlockSpec((HALO, D), lambda i: (jnp.minimum((i + 1) * hb, n_halo_blocks - 1), 0)),
            pl.BlockSpec((1, CW, D), lambda i: (idx, 0, 0)),
            _layer_vec_spec(idx),
            _layer_vec_spec(idx),
            _layer_vec_spec(idx),
        ],
        out_specs=pl.BlockSpec((TS, D), lambda i: (i, 0)),
        out_shape=jax.ShapeDtypeStruct((t_rows, D), BF16),
        scratch_shapes=[pltpu.VMEM((TS + 2 * HALO, D), F32), pltpu.VMEM((TS, D), F32),
                        pltpu.VMEM((SUBLANES - 1, TS + SH_EXTRA, LANES), F32)],
        compiler_params=_params("arbitrary"),
        name="dwconv_ln_silu",
    )(u, u, u, w_dw, b_dw, ln_g, ln_b)


def _router_body(x_ref, g_ref, sh_ref, sc_ref, wr_ref, hm_ref, info_ref):
    hm = _modulate(x_ref[...], g_ref[0], sh_ref[0], sc_ref[0])
    for s in range(SLABS):
        hm_ref[pl.ds(s, ROUTER_TM, stride=SLABS), :] = hm[:, s * LANES:(s + 1) * LANES]
    logits = jnp.dot(hm, wr_ref[...], precision=lax.Precision.HIGHEST, preferred_element_type=F32)
    lane = lax.broadcasted_iota(jnp.int32, logits.shape, 1).astype(F32)
    ninf = -jnp.inf
    lg = jnp.where(lane < E, logits, ninf)
    v1 = jnp.max(lg, axis=-1, keepdims=True)
    i1 = jnp.min(jnp.where(lg == v1, lane, float(LANES)), axis=-1, keepdims=True)
    lg2 = jnp.where(lane == i1, ninf, lg)
    v2 = jnp.max(lg2, axis=-1, keepdims=True)
    i2 = jnp.min(jnp.where(lg2 == v2, lane, float(LANES)), axis=-1, keepdims=True)
    e2 = jnp.exp(v2 - v1)
    w1 = 1.0 / (1.0 + e2)
    w2 = e2 * w1
    info_ref[...] = jnp.where(lane == 0, i1, jnp.where(lane == 1, i2,
                              jnp.where(lane == 2, w1, jnp.where(lane == 3, w2, 0.0))))


def _router(h, mods, g, layer, w_router_pad, t_rows):
    tm = ROUTER_TM
    return pl.pallas_call(
        _router_body,
        grid=(t_rows // tm,),
        in_specs=[
            pl.BlockSpec((tm, D), lambda i: (i, 0)),
            _layer_vec_spec(layer),
            _mod_spec(layer, 3, tm),
            _mod_spec(layer, 4, tm),
            pl.BlockSpec((D, LANES), lambda i: (0, 0)),
        ],
        out_specs=[pl.BlockSpec((tm * SLABS, LANES), lambda i: (i, 0)),
                   pl.BlockSpec((tm, LANES), lambda i: (i, 0))],
        out_shape=[jax.ShapeDtypeStruct((t_rows * SLABS, LANES), F32),
                   jax.ShapeDtypeStruct((t_rows, LANES), F32)],
        compiler_params=_params("arbitrary"),
        name="router",
    )(h, g, mods, mods, w_router_pad)


def _row_gather_step(i, n_steps, idx_ref, idx_next_ref, src_ref, buf, sem, rows, consume):
    def row_copy(src_row, r, slot):
        return pltpu.make_async_copy(
            src_ref.at[pl.ds(pl.multiple_of(src_row * SLABS, SLABS), SLABS)],
            buf.at[slot, pl.ds(pl.multiple_of(r * ROW_PITCH, SUBLANES), SLABS)],
            sem.at[slot])

    def issue(ref, slot):
        def body(r, carry):
            row_copy(ref[0, 0, r], r, slot).start()
            return carry
        lax.fori_loop(0, rows, body, 0, unroll=8)

    def step(slot):
        if slot == 0:
            @pl.when(i == 0)
            def _():
                issue(idx_ref, 0)

        @pl.when(i + 1 < n_steps)
        def _():
            issue(idx_next_ref, 1 - slot)

        pltpu.make_async_copy(src_ref.at[pl.ds(0, rows * SLABS)], buf.at[slot, pl.ds(0, rows * SLABS)],
                              sem.at[slot]).wait()
        consume(slot)

    @pl.when(i % 2 == 0)
    def _():
        step(0)

    @pl.when(i % 2 == 1)
    def _():
        step(1)


def _gather_body(idx_ref, idx_next_ref, src_ref, o_ref, buf, sem):
    def consume(slot):
        for s in range(SLABS):
            o_ref[:, s * LANES:(s + 1) * LANES] = buf[slot, pl.ds(s, GATHER_ROWS, stride=ROW_PITCH), :].astype(BF16)

    _row_gather_step(pl.program_id(0), pl.num_programs(0), idx_ref, idx_next_ref, src_ref, buf, sem,
                     GATHER_ROWS, consume)


def _gather_rows(src_slabs, idx):
    n = idx.shape[0]
    assert n % GATHER_ROWS == 0
    nsteps = n // GATHER_ROWS
    idx3 = idx.reshape(nsteps, 1, GATHER_ROWS)
    return pl.pallas_call(
        _gather_body,
        grid=(nsteps,),
        in_specs=[pl.BlockSpec((1, 1, GATHER_ROWS), lambda i: (i, 0, 0), memory_space=pltpu.SMEM),
                  pl.BlockSpec((1, 1, GATHER_ROWS), lambda i: (jnp.minimum(i + 1, nsteps - 1), 0, 0),
                               memory_space=pltpu.SMEM),
                  pl.BlockSpec(memory_space=pl.ANY)],
        out_specs=pl.BlockSpec((GATHER_ROWS, D), lambda i: (i, 0)),
        out_shape=jax.ShapeDtypeStruct((n, D), BF16),
        scratch_shapes=[pltpu.VMEM((2, GATHER_ROWS * ROW_PITCH, LANES), F32), pltpu.SemaphoreType.DMA((2,))],
        compiler_params=_params("arbitrary"),
        name="moe_gather",
    )(idx3, idx3, src_slabs)


def _moe_up_body(te_ref, nu_ref, x_ref, wg_ref, wu_ref, o_ref, wg_bf, wu_bf):
    i = pl.program_id(1)

    @pl.when(i < nu_ref[0])
    def _():
        @pl.when(jnp.logical_or(i == 0, te_ref[i] != te_ref[jnp.maximum(i - 1, 0)]))
        def _():
            wg_bf[...] = wg_ref[0, 0].astype(BF16)
            wu_bf[...] = wu_ref[0, 0].astype(BF16)

        x = x_ref[...]
        for c in range(BN_UP // MXU_COLS):
            cs = slice(c * MXU_COLS, (c + 1) * MXU_COLS)
            a = jnp.dot(x, wg_bf[:, cs], preferred_element_type=F32)
            b = jnp.dot(x, wu_bf[:, cs], preferred_element_type=F32)
            o_ref[:, cs] = ((a * _sigmoid(a)) * b).astype(BF16)


def _moe_down_body(te_ref, nu_ref, a_ref, wd_ref, o_ref):
    @pl.when(pl.program_id(0) < nu_ref[0])
    def _():
        a = a_ref[...]
        for c in range(D // MXU_COLS):
            y = jnp.dot(a, wd_ref[0, :, c * MXU_COLS:(c + 1) * MXU_COLS], preferred_element_type=F32)
            for k in range(MXU_COLS // LANES):
                s = c * (MXU_COLS // LANES) + k
                o_ref[pl.ds(s, TME_DOWN, stride=SLABS), :] = y[:, k * LANES:(k + 1) * LANES]


def _moe_experts(xs, w_gu_all, moe_idx, w_down, tile_expert, n_used):
    n_tiles = xs.shape[0] // TME_UP
    nj = DFF // BN_UP

    def tile(i, nu):
        return jnp.minimum(i, nu[0] - 1)

    act = pl.pallas_call(
        _moe_up_body,
        grid_spec=pltpu.PrefetchScalarGridSpec(
            num_scalar_prefetch=2,
            grid=(nj, n_tiles),
            in_specs=[
                pl.BlockSpec((TME_UP, D), lambda j, i, te, nu: (tile(i, nu), 0)),
                pl.BlockSpec((1, 1, D, BN_UP), lambda j, i, te, nu: (moe_idx, te[tile(i, nu)], 0, j)),
                pl.BlockSpec((1, 1, D, BN_UP), lambda j, i, te, nu: (moe_idx, te[tile(i, nu)], 0, nj + j)),
            ],
            out_specs=pl.BlockSpec((TME_UP, BN_UP), lambda j, i, te, nu: (tile(i, nu), j)),
            scratch_shapes=[pltpu.VMEM((D, BN_UP), BF16), pltpu.VMEM((D, BN_UP), BF16)],
        ),
        out_shape=jax.ShapeDtypeStruct((n_tiles * TME_UP, DFF), BF16),
        compiler_params=_params("arbitrary", "arbitrary"),
        name="moe_up",
    )(tile_expert, n_used, xs, w_gu_all, w_gu_all)

    split = TME_UP // TME_DOWN
    return pl.pallas_call(
        _moe_down_body,
        grid_spec=pltpu.PrefetchScalarGridSpec(
            num_scalar_prefetch=2,
            grid=(n_tiles * split,),
            in_specs=[
                pl.BlockSpec((TME_DOWN, DFF), lambda i, te, nu: (tile(i, nu), 0)),
                pl.BlockSpec((1, DFF, D), lambda i, te, nu: (te[tile(i, nu)], 0, 0),
                             pipeline_mode=pl.Buffered(1)),
            ],
            out_specs=pl.BlockSpec((TME_DOWN * SLABS, LANES), lambda i, te, nu: (tile(i, nu), 0)),
        ),
        out_shape=jax.ShapeDtypeStruct((n_tiles * TME_UP * SLABS, LANES), F32),
        compiler_params=_params("arbitrary"),
        name="moe_down",
    )(jnp.repeat(tile_expert, split), n_used * split, act, w_down)


def _routing_tables(info, t_rows, n_tiles):
    idx = info[:, :2].astype(jnp.int32)
    flat_e = idx.T.reshape(-1)
    onehot = (flat_e[:, None] == jnp.arange(E, dtype=jnp.int32)[None, :]).astype(jnp.int32)
    csum = jnp.cumsum(onehot, axis=0)
    rank = jnp.sum((csum - onehot) * onehot, axis=1)
    counts = csum[-1]
    padded = ((counts + TME - 1) // TME) * TME
    ends = jnp.cumsum(padded)
    starts = ends - padded
    pos = starts[flat_e] + rank
    n_rows = n_tiles * TME
    tile_start = jnp.arange(n_tiles, dtype=jnp.int32) * TME
    tile_expert = jnp.minimum(jnp.sum((tile_start[:, None] >= ends[None, :]).astype(jnp.int32), axis=1), E - 1)
    n_used = (ends[-1] // TME).astype(jnp.int32).reshape(1)
    token_of_pos = jnp.zeros((n_rows,), jnp.int32).at[pos].set(
        jnp.arange(2 * t_rows, dtype=jnp.int32) % t_rows)
    return pos.astype(jnp.int32), token_of_pos, tile_expert.astype(jnp.int32), n_used


def _combine_body(idx_ref, idx_next_ref, ys_ref, h_ref, info_ref, gate_ref, *rest, final):
    if final:
        fg_ref, o_ref, buf, sem = rest
    else:
        o_ref, buf, sem = rest

    def consume(slot):
        w0 = jnp.broadcast_to(info_ref[:, 2:3], (COMBINE_TM, LANES))
        w1 = jnp.broadcast_to(info_ref[:, 3:4], (COMBINE_TM, LANES))
        for s in range(SLABS):
            cols = slice(s * LANES, (s + 1) * LANES)
            y = (w0 * buf[slot, pl.ds(s, COMBINE_TM, stride=ROW_PITCH), :]
                 + w1 * buf[slot, pl.ds(COMBINE_TM * ROW_PITCH + s, COMBINE_TM, stride=ROW_PITCH), :])
            o_ref[:, cols] = h_ref[:, cols] + gate_ref[0, :, cols] * y
        if final:
            h = o_ref[...]
            o_ref[...] = (h * lax.rsqrt(jnp.mean(h * h, axis=-1, keepdims=True) + EPS)) * fg_ref[...]

    _row_gather_step(pl.program_id(0), pl.num_programs(0), idx_ref, idx_next_ref, ys_ref, buf, sem,
                     2 * COMBINE_TM, consume)


def _combine(h, ys_slabs, pos, info, mods, layer, t_rows, final_g):
    tm = COMBINE_TM
    nt = t_rows // tm
    final = final_g is not None
    idx3 = pos.reshape(2, nt, tm).transpose(1, 0, 2).reshape(nt, 1, 2 * tm)
    in_specs = [pl.BlockSpec((1, 1, 2 * tm), lambda i: (i, 0, 0), memory_space=pltpu.SMEM),
                pl.BlockSpec((1, 1, 2 * tm), lambda i: (jnp.minimum(i + 1, nt - 1), 0, 0),
                             memory_space=pltpu.SMEM),
                pl.BlockSpec(memory_space=pl.ANY),
                pl.BlockSpec((tm, D), lambda i: (i, 0)),
                pl.BlockSpec((tm, LANES), lambda i: (i, 0)),
                _mod_spec(layer, 5, tm)]
    args = [idx3, idx3, ys_slabs, h, info, mods]
    if final:
        in_specs.append(pl.BlockSpec((1, D), lambda i: (0, 0)))
        args.append(final_g.reshape(1, D))
    return pl.pallas_call(
        functools.partial(_combine_body, final=final),
        grid=(nt,),
        in_specs=in_specs,
        out_specs=pl.BlockSpec((tm, D), lambda i: (i, 0)),
        out_shape=jax.ShapeDtypeStruct((t_rows, D), F32),
        scratch_shapes=[pltpu.VMEM((2, 2 * tm * ROW_PITCH, LANES), F32), pltpu.SemaphoreType.DMA((2,))],
        compiler_params=_params("arbitrary"),
        name="moe_combine",
    )(*args)


def _rope_tables():
    rows = S // GRID_W
    row = jnp.repeat(jnp.arange(rows, dtype=F32), GRID_W)
    col = jnp.tile(jnp.arange(GRID_W, dtype=F32), rows)
    n_freq = HD // 4
    inv_freq = ROPE_BASE ** (-jnp.arange(n_freq, dtype=F32) / n_freq)
    ang = jnp.concatenate([row[:, None] * inv_freq, col[:, None] * inv_freq], axis=-1)
    ang = jnp.concatenate([ang, ang], axis=-1)
    sign = jnp.where(jnp.arange(HD) < HD // 2, -1.0, 1.0).astype(F32)
    cos = jnp.concatenate([jnp.cos(ang), jnp.ones((TM, HD), F32)], axis=0)
    sin = jnp.concatenate([jnp.sin(ang) * sign, jnp.zeros((TM, HD), F32)], axis=0)
    scale = HD ** -0.5 * math.log2(math.e)
    return jnp.stack([cos * scale, cos]), jnp.stack([sin * scale, sin])


def kernel(x, c, ctx, c_ctx, ada_w, ada_b, norm_mix_g, norm_ffn_g, attn_w_qkv, attn_lambda, attn_subln_g, attn_w_o, conv_w_in, conv_b_in, conv_w_dw, conv_b_dw, conv_ln_g, conv_ln_b, conv_w_out, conv_b_out, ffn_w_gu, ffn_w_down, moe_router, moe_w_gu, moe_w_down, final_g):
    h = jnp.concatenate([x.reshape(T_LAT, D), ctx.reshape(T_CTX, D)], axis=0)
    cond = jnp.concatenate([c, c_ctx[None, :], jnp.zeros((COND_ROWS - B - 1, D), F32)], axis=0)
    mods = _ada_mods(cond, ada_w, ada_b)
    cos_t, sin_t = _rope_tables()
    mix_g = norm_mix_g.reshape(DEPTH, 1, D)
    ffn_g = norm_ffn_g.reshape(DEPTH, 1, D)

    for i in range(DEPTH):
        mix_idx = i // 2
        ffn_idx = i // 2
        ctx_live = i < LAST_CTX_READER
        t_rows = T_ALL if ctx_live else T_LAT
        if i % 2 == 0:
            lambda_init = 0.8 - 0.6 * math.exp(-0.3 * i)
            qkv = _qkv_proj(h, mods, mix_g, i, attn_w_qkv[mix_idx].astype(BF16), cos_t, sin_t)
            o = _attention(qkv, attn_lambda, attn_subln_g, mix_idx, lambda_init, ctx_live)
            h = _resid_proj(o, attn_w_o[mix_idx].astype(BF16), None, h, mods, i, 2, t_rows, "attn_out")
        else:
            u = _glu_proj(h, mods, mix_g, i, conv_w_in[mix_idx].astype(BF16),
                          (mix_idx, conv_b_in.reshape(-1, 1, 2 * D)), t_rows, "glu", F32, "conv_in")
            z = _conv_module(u, conv_w_dw, conv_b_dw.reshape(-1, 1, D), conv_ln_g.reshape(-1, 1, D),
                             conv_ln_b.reshape(-1, 1, D), mix_idx, t_rows)
            h = _resid_proj(z, conv_w_out[mix_idx].astype(BF16), (mix_idx, conv_b_out.reshape(-1, 1, D)),
                            h, mods, i, 2, t_rows, "conv_out")
        if i % 2 == 0:
            act = _glu_proj(h, mods, ffn_g, i, ffn_w_gu[ffn_idx].astype(BF16), None, t_rows,
                            "swiglu", BF16, "ffn_up")
            h = _resid_proj(act, ffn_w_down[ffn_idx].astype(BF16), None, h, mods, i, 5, t_rows, "ffn_down")
        else:
            n_tiles = N_TILES_E if ctx_live else N_TILES_E_LAT
            w_r = jnp.zeros((D, LANES), F32).at[:, :E].set(moe_router[ffn_idx])
            hm_slabs, info = _router(h, mods, ffn_g, i, w_r, t_rows)
            pos, token_of_pos, tile_expert, n_used = _routing_tables(info, t_rows, n_tiles)
            xs = _gather_rows(hm_slabs, token_of_pos)
            ys_slabs = _moe_experts(xs, moe_w_gu, ffn_idx, moe_w_down[ffn_idx].astype(BF16),
                                    tile_expert, n_used)
            h = _combine(h, ys_slabs, pos, info, mods, i, t_rows, final_g if i == DEPTH - 1 else None)
    return h.reshape(B, S, D)
```

```python
import functools
import math

import jax
import jax.numpy as jnp
from jax import lax
from jax.experimental import pallas as pl
from jax.experimental.pallas import tpu as pltpu

F32 = jnp.float32
BF16 = jnp.bfloat16

D = 2048
B = 16
S = 2048
L = 256
DEPTH = 4
GRID_W = 64
H = 8
HD = 128
VD = 2 * HD
ROPE_BASE = 10000.0
CW = 31
CPAD = (CW - 1) // 2
DFF = 5632
E = 8
NMOD = 6
EPS = 1e-6

T_LAT = B * S
T_CTX = B * L
T_ALL = T_LAT + T_CTX
COND_ROWS = 24
CTX_COND_ROW = B
LAST_CTX_READER = 2

V7X_VMEM_LIMIT_BYTES = 56 * 1024 * 1024
LANES = 128
SUBLANES = 8
HALO = 16
SH_EXTRA = SUBLANES * ((CW - 1 + HALO - CPAD) // SUBLANES)

TM = 1024
BN = 512
BN_QKV = 1024
TM_RESID = 512
TQ = 1024
TQ_SUB = 128
TS = 256
MXU_COLS = 256
TME = 1024
TME_UP = TME
TME_DOWN = 512
BN_UP = 512
SLABS = D // LANES
ROW_PITCH = SLABS + SUBLANES
ROUTER_TM = 512
GATHER_ROWS = 512
COMBINE_TM = 256
N_TILES_E = (2 * T_ALL + E * (TME - 1) + TME - 1) // TME
N_TILES_E_LAT = (2 * T_LAT + E * (TME - 1) + TME - 1) // TME


def _params(*sem):
    return pltpu.CompilerParams(dimension_semantics=sem, vmem_limit_bytes=V7X_VMEM_LIMIT_BYTES)


def _sigmoid(x):
    return 1.0 / (1.0 + jnp.exp(-x))


def _modulate(x, g, shift, scale):
    xn = x * lax.rsqrt(jnp.mean(x * x, axis=-1, keepdims=True) + EPS)
    return (xn * g) * (1.0 + scale) + shift


def _modulate_into(x_ref, g_ref, sh_ref, sc_ref, out_ref):
    g, shift, scale = g_ref[0], sh_ref[0], sc_ref[0]
    rb = 2 * SUBLANES

    def body(r, carry):
        rows = pl.ds(pl.multiple_of(r * rb, rb), rb)
        out_ref[rows, :] = _modulate(x_ref[rows, :], g, shift, scale).astype(out_ref.dtype)
        return carry

    lax.fori_loop(0, x_ref.shape[0] // rb, body, 0, unroll=8)


def _cond_row(i, tm):
    return jnp.where(i * tm < T_LAT, (i * tm) // S, CTX_COND_ROW)


def _mod_spec(layer, which, tm, tile_off=0):
    def imap(i, *rest):
        return ((layer * COND_ROWS + _cond_row(i + tile_off, tm)) * NMOD + which, 0, 0)
    return pl.BlockSpec((1, 1, D), imap)


def _layer_vec_spec(idx):
    return pl.BlockSpec((1, 1, D), lambda *grid: (idx, 0, 0))


def _ada_body(c_ref, w_ref, b_ref, o_ref):
    x = c_ref[...]
    s = (x * _sigmoid(x)).astype(BF16)
    o_ref[0] = jnp.dot(s, w_ref[0].astype(BF16), preferred_element_type=F32) + b_ref[0]


def _ada_mods(cond, ada_w, ada_b):
    bn = 1024
    n = NMOD * D
    out = pl.pallas_call(
        _ada_body,
        grid=(DEPTH, n // bn),
        in_specs=[
            pl.BlockSpec((COND_ROWS, D), lambda l, j: (0, 0)),
            pl.BlockSpec((1, D, bn), lambda l, j: (l, 0, j)),
            pl.BlockSpec((1, 1, bn), lambda l, j: (l, 0, j)),
        ],
        out_specs=pl.BlockSpec((1, COND_ROWS, bn), lambda l, j: (l, 0, j)),
        out_shape=jax.ShapeDtypeStruct((DEPTH, COND_ROWS, n), F32),
        compiler_params=_params("arbitrary", "arbitrary"),
        name="ada_mods",
    )(cond, ada_w, ada_b.reshape(DEPTH, 1, n))
    return out.reshape(DEPTH * COND_ROWS * NMOD, 1, D)


def _qkv_body(x_ref, g_ref, sh_ref, sc_ref, w_ref, cos_ref, sin_ref, *rest, n_rope):
    o_ref, hm_ref = rest[-2:]
    j = pl.program_id(1)

    @pl.when(j == 0)
    def _():
        _modulate_into(x_ref, g_ref, sh_ref, sc_ref, hm_ref)

    hm = hm_ref[...]

    def chunk(c):
        return jnp.dot(hm, w_ref[:, c * MXU_COLS:(c + 1) * MXU_COLS], preferred_element_type=F32)

    @pl.when(j < n_rope)
    def _():
        cos = cos_ref[0]
        sin = sin_ref[0]
        for c in range(BN_QKV // MXU_COLS):
            acc = chunk(c)
            for k in range(MXU_COLS // HD):
                xc = acc[:, k * HD:(k + 1) * HD]
                col = c * MXU_COLS + k * HD
                o_ref[:, col:col + HD] = (xc * cos + pltpu.roll(xc, HD // 2, 1) * sin).astype(BF16)

    @pl.when(j >= n_rope)
    def _():
        for c in range(BN_QKV // MXU_COLS):
            o_ref[:, c * MXU_COLS:(c + 1) * MXU_COLS] = chunk(c).astype(BF16)


def _qkv_proj(h, mods, g, layer, w_qkv, cos_t, sin_t, row_off=0, out=None):
    off = row_off // TM
    nq = D // BN_QKV
    tiles_per_seq = S // TM

    def rope_map(i, j):
        gi = i + off
        return (jnp.where(j < nq, 0, 1),
                jnp.where(gi * TM < T_LAT, gi % tiles_per_seq, tiles_per_seq), 0)

    in_specs = [
        pl.BlockSpec((TM, D), lambda i, j: (i, 0)),
        _layer_vec_spec(layer),
        _mod_spec(layer, 0, TM, off),
        _mod_spec(layer, 1, TM, off),
        pl.BlockSpec((D, BN_QKV), lambda i, j: (0, j)),
        pl.BlockSpec((1, TM, HD), rope_map),
        pl.BlockSpec((1, TM, HD), rope_map),
    ]
    args = [h, g, mods, mods, w_qkv, cos_t, sin_t]
    aliases = {}
    if out is not None:
        in_specs.append(pl.BlockSpec(memory_space=pl.ANY))
        args.append(out)
        aliases = {len(args) - 1: 0}
    return pl.pallas_call(
        functools.partial(_qkv_body, n_rope=2 * nq),
        grid=(h.shape[0] // TM, 3 * D // BN_QKV),
        in_specs=in_specs,
        out_specs=pl.BlockSpec((TM, BN_QKV), lambda i, j: (i + off, j)),
        out_shape=jax.ShapeDtypeStruct((T_ALL, 3 * D), BF16),
        scratch_shapes=[pltpu.VMEM((TM, D), BF16)],
        input_output_aliases=aliases,
        compiler_params=_params("arbitrary", "arbitrary"),
        name="qkv_proj",
    )(*args)


def _attn_body(lam_ref, g_ref, q_ref, kc_ref, vc_ref, *rest, lambda_init, with_lat):
    if with_lat:
        k_ref, v_ref, o_ref = rest
    else:
        o_ref = rest[-1]
    lp = lam_ref[0]
    lam = (jnp.exp(jnp.sum(lp[0:1] * lp[1:2], axis=-1, keepdims=True))
           - jnp.exp(jnp.sum(lp[2:3] * lp[3:4], axis=-1, keepdims=True)) + lambda_init)
    nt_dims = (((1,), (1,)), ((), ()))
    n_rows = q_ref.shape[0]
    sub = min(n_rows, TQ_SUB)
    def scores(t):
        out = []
        for c in range(2):
            sl = slice(c * HD, (c + 1) * HD)
            qc = q_ref[t * sub:(t + 1) * sub, sl]
            s_c = lax.dot_general(qc, kc_ref[:, sl], nt_dims, preferred_element_type=F32)
            s_l = lax.dot_general(qc, k_ref[:, sl], nt_dims, preferred_element_type=F32) if with_lat else None
            out.append((s_c, s_l))
        return out

    def weights(sc):
        parts = []
        for s_c, s_l in sc:
            m = jnp.max(s_c, axis=-1, keepdims=True)
            if with_lat:
                m = jnp.maximum(m, jnp.max(s_l, axis=-1, keepdims=True))
                p_l = jnp.exp2(s_l - m)
            p_c = jnp.exp2(s_c - m)
            den = jnp.sum(p_c, axis=-1, keepdims=True)
            if with_lat:
                den = den + jnp.sum(p_l, axis=-1, keepdims=True)
            parts.append((p_c, p_l if with_lat else None, den))
        (p1c, p1l, den1), (p2c, p2l, den2) = parts
        r = lam * den1 / den2
        a_c = (p1c - r * p2c).astype(BF16)
        a_l = (p1l - r * p2l).astype(BF16) if with_lat else None
        return a_c, a_l, den1

    def output(t, w):
        a_c, a_l, den1 = w
        o = jnp.dot(a_c, vc_ref[...], preferred_element_type=F32)
        if with_lat:
            o = o + jnp.dot(a_l, v_ref[...], preferred_element_type=F32)
        o = o / den1
        o = o * lax.rsqrt(jnp.mean(o * o, axis=-1, keepdims=True) + EPS) * g_ref[0] * (1.0 - lambda_init)
        o_ref[t * sub:(t + 1) * sub, :] = o.astype(BF16)

    n_sub = n_rows // sub
    nxt = scores(0)
    for t in range(n_sub):
        cur = nxt
        if t + 1 < n_sub:
            nxt = scores(t + 1)
        output(t, weights(cur))


def _attention(qkv, lam_p, subln_g, mix_idx, lambda_init, ctx_out):
    t_out = T_ALL if ctx_out else T_LAT
    nqt = S // TQ
    kcol, vcol = D // VD, 2 * D // VD
    ctx_blk = T_LAT // L
    common = [
        pl.BlockSpec((1, 4, HD), lambda b, h, t: (mix_idx, 0, 0)),
        pl.BlockSpec((1, 1, VD), lambda b, h, t: (mix_idx, 0, 0)),
    ]
    ctx_kv = [
        pl.BlockSpec((L, VD), lambda b, h, t: (ctx_blk + b, kcol + h)),
        pl.BlockSpec((L, VD), lambda b, h, t: (ctx_blk + b, vcol + h)),
    ]
    g3 = subln_g.reshape(-1, 1, VD)
    o = pl.pallas_call(
        functools.partial(_attn_body, lambda_init=lambda_init, with_lat=True),
        grid=(B, H, nqt),
        in_specs=common + [pl.BlockSpec((TQ, VD), lambda b, h, t: (b * nqt + t, h))] + ctx_kv + [
            pl.BlockSpec((S, VD), lambda b, h, t: (b, kcol + h)),
            pl.BlockSpec((S, VD), lambda b, h, t: (b, vcol + h)),
        ],
        out_specs=pl.BlockSpec((TQ, VD), lambda b, h, t: (b * nqt + t, h)),
        out_shape=jax.ShapeDtypeStruct((t_out, D), BF16),
        compiler_params=_params("arbitrary", "arbitrary", "arbitrary"),
        name="attn_lat",
    )(lam_p, g3, qkv, qkv, qkv, qkv, qkv)
    if not ctx_out:
        return o
    return pl.pallas_call(
        functools.partial(_attn_body, lambda_init=lambda_init, with_lat=False),
        grid=(B, H, 1),
        in_specs=common + [pl.BlockSpec((L, VD), lambda b, h, t: (ctx_blk + b, h))] + ctx_kv + [
            pl.BlockSpec(memory_space=pl.ANY),
        ],
        out_specs=pl.BlockSpec((L, VD), lambda b, h, t: (ctx_blk + b, h)),
        out_shape=jax.ShapeDtypeStruct((t_out, D), BF16),
        input_output_aliases={5: 0},
        compiler_params=_params("arbitrary", "arbitrary", "arbitrary"),
        name="attn_ctx",
    )(lam_p, g3, qkv, qkv, qkv, o)


def _resid_body(x_ref, w_ref, *rest, has_bias):
    o_ref = rest[-1]
    if has_bias:
        b_ref, r_ref, gate_ref = rest[:3]
    else:
        r_ref, gate_ref = rest[:2]
    x = x_ref[...]
    for c in range(D // MXU_COLS):
        cs = slice(c * MXU_COLS, (c + 1) * MXU_COLS)
        y = jnp.dot(x, w_ref[:, cs], preferred_element_type=F32)
        if has_bias:
            y = y + b_ref[0, :, cs]
        o_ref[:, cs] = r_ref[:, cs] + gate_ref[0, :, cs] * y


def _resid_proj(x, w, bias, resid, mods, layer, which, t_out, name, n_rows=None, row_off=0, out=None):
    k = x.shape[1]
    tm = TM_RESID
    off = row_off // tm
    n_rows = t_out if n_rows is None else n_rows
    has_bias = bias is not None
    in_specs = [pl.BlockSpec((tm, k), lambda i: (i + off, 0)),
                pl.BlockSpec((k, D), lambda i: (0, 0), pipeline_mode=pl.Buffered(1))]
    args = [x, w]
    if has_bias:
        idx, arr = bias
        in_specs.append(_layer_vec_spec(idx))
        args.append(arr)
    in_specs += [pl.BlockSpec((tm, D), lambda i: (i, 0)), _mod_spec(layer, which, tm, off)]
    args += [resid, mods]
    aliases = {}
    if out is not None:
        in_specs.append(pl.BlockSpec(memory_space=pl.ANY))
        args.append(out)
        aliases = {len(args) - 1: 0}
    return pl.pallas_call(
        functools.partial(_resid_body, has_bias=has_bias),
        grid=(n_rows // tm,),
        in_specs=in_specs,
        out_specs=pl.BlockSpec((tm, D), lambda i: (i + off, 0)),
        out_shape=jax.ShapeDtypeStruct((t_out, D), F32),
        input_output_aliases=aliases,
        compiler_params=_params("arbitrary"),
        name=name,
    )(*args)


def _glu_body(x_ref, g_ref, sh_ref, sc_ref, wa_ref, wb_ref, *rest, kind):
    if kind == "glu":
        ba_ref, bb_ref, o_ref, hm_ref = rest
    else:
        o_ref, hm_ref = rest

    @pl.when(pl.program_id(1) == 0)
    def _():
        _modulate_into(x_ref, g_ref, sh_ref, sc_ref, hm_ref)

    hm = hm_ref[...]
    for c in range(BN // MXU_COLS):
        cs = slice(c * MXU_COLS, (c + 1) * MXU_COLS)
        a = jnp.dot(hm, wa_ref[:, cs], preferred_element_type=F32)
        b = jnp.dot(hm, wb_ref[:, cs], preferred_element_type=F32)
        if kind == "glu":
            o = (a + ba_ref[0, :, cs]) * _sigmoid(b + bb_ref[0, :, cs])
        else:
            o = (a * _sigmoid(a)) * b
        o_ref[:, cs] = o.astype(o_ref.dtype)


def _glu_proj(h, mods, g, layer, w, bias, t_rows, kind, out_dtype, name):
    nh = w.shape[1] // 2
    nj = nh // BN
    in_specs = [
        pl.BlockSpec((TM, D), lambda i, j: (i, 0)),
        _layer_vec_spec(layer),
        _mod_spec(layer, 3 if kind == "swiglu" else 0, TM),
        _mod_spec(layer, 4 if kind == "swiglu" else 1, TM),
        pl.BlockSpec((D, BN), lambda i, j: (0, j)),
        pl.BlockSpec((D, BN), lambda i, j: (0, nj + j)),
    ]
    args = [h, g, mods, mods, w, w]
    if kind == "glu":
        idx, arr = bias
        in_specs += [pl.BlockSpec((1, 1, BN), lambda i, j: (idx, 0, j)),
                     pl.BlockSpec((1, 1, BN), lambda i, j: (idx, 0, nj + j))]
        args += [arr, arr]
    return pl.pallas_call(
        functools.partial(_glu_body, kind=kind),
        grid=(t_rows // TM, nj),
        in_specs=in_specs,
        out_specs=pl.BlockSpec((TM, BN), lambda i, j: (i, j)),
        out_shape=jax.ShapeDtypeStruct((t_rows, nh), out_dtype),
        scratch_shapes=[pltpu.VMEM((TM, D), BF16)],
        compiler_params=_params("arbitrary", "arbitrary"),
        name=name,
    )(*args)


def _conv_body(prev_ref, cur_ref, next_ref, w_ref, b_ref, lg_ref, lb_ref, o_ref, ext_ref, y_ref, sh_ref):
    i = pl.program_id(0)
    tiles_per_seq = S // TS
    is_ctx = i * TS >= T_LAT
    first = jnp.logical_or(is_ctx, i % tiles_per_seq == 0)
    last = jnp.logical_or(is_ctx, i % tiles_per_seq == tiles_per_seq - 1)
    ext_ref[0:HALO, :] = jnp.where(first, 0.0, prev_ref[...])
    ext_ref[HALO:HALO + TS, :] = cur_ref[...]
    ext_ref[HALO + TS:2 * HALO + TS, :] = jnp.where(last, 0.0, next_ref[...])

    rc = 64
    off = HALO - CPAD

    def chan(c, carry):
        cs = pl.ds(pl.multiple_of(c * LANES, LANES), LANES)
        for d in range(1, SUBLANES):
            sh_ref[d - 1] = ext_ref[pl.ds(d, TS + SH_EXTRA), cs]
        bias = b_ref[0, :, cs]
        accs = [jnp.broadcast_to(bias, (rc, LANES)) for _ in range(TS // rc)]
        for k in range(CW):
            q, d = divmod(k + off, SUBLANES)
            wk = jnp.broadcast_to(w_ref[0, k:k + 1, cs], (rc, LANES))
            for r in range(TS // rc):
                rows = pl.ds(r * rc + q * SUBLANES, rc)
                tap = ext_ref[rows, cs] if d == 0 else sh_ref[d - 1, rows, :]
                accs[r] = accs[r] + wk * tap
        for r in range(TS // rc):
            y_ref[pl.ds(r * rc, rc), cs] = accs[r]
        return carry

    lax.fori_loop(0, D // LANES, chan, 0)

    y = y_ref[...]
    yc = y - jnp.mean(y, axis=-1, keepdims=True)
    var = jnp.mean(yc * yc, axis=-1, keepdims=True)
    z = yc * lax.rsqrt(var + EPS) * lg_ref[0] + lb_ref[0]
    o_ref[...] = (z * _sigmoid(z)).astype(BF16)


def _conv_module(u, w_dw, b_dw, ln_g, ln_b, idx, t_rows):
    assert L == TS and TS % HALO == 0 and HALO >= CPAD
    hb = TS // HALO
    n_halo_blocks = u.shape[0] // HALO
    return pl.pallas_call(
        _conv_body,
        grid=(t_rows // TS,),
        in_specs=[
            pl.BlockSpec((HALO, D), lambda i: (jnp.maximum(i * hb - 1, 0), 0)),
            pl.BlockSpec((TS, D), lambda i: (i, 0)),
            pl.BlockSpec((HALO, D), lambda i: (jnp.minimum((i + 1) * hb, n_halo_blocks - 1), 0)),
            pl.BlockSpec((1, CW, D), lambda i: (idx, 0, 0)),
            _layer_vec_spec(idx),
            _layer_vec_spec(idx),
            _layer_vec_spec(idx),
        ],
        out_specs=pl.BlockSpec((TS, D), lambda i: (i, 0)),
        out_shape=jax.ShapeDtypeStruct((t_rows, D), BF16),
        scratch_shapes=[pltpu.VMEM((TS + 2 * HALO, D), F32), pltpu.VMEM((TS, D), F32),
                        pltpu.VMEM((SUBLANES - 1, TS + SH_EXTRA, LANES), F32)],
        compiler_params=_params("arbitrary"),
        name="dwconv_ln_silu",
    )(u, u, u, w_dw, b_dw, ln_g, ln_b)


def _router_body(x_ref, g_ref, sh_ref, sc_ref, wr_ref, hm_ref, info_ref):
    hm = _modulate(x_ref[...], g_ref[0], sh_ref[0], sc_ref[0])
    for s in range(SLABS):
        hm_ref[pl.ds(s, ROUTER_TM, stride=SLABS), :] = hm[:, s * LANES:(s + 1) * LANES]
    logits = jnp.dot(hm, wr_ref[...], precision=lax.Precision.HIGHEST, preferred_element_type=F32)
    lane = lax.broadcasted_iota(jnp.int32, logits.shape, 1).astype(F32)
    ninf = -jnp.inf
    lg = jnp.where(lane < E, logits, ninf)
    v1 = jnp.max(lg, axis=-1, keepdims=True)
    i1 = jnp.min(jnp.where(lg == v1, lane, float(LANES)), axis=-1, keepdims=True)
    lg2 = jnp.where(lane == i1, ninf, lg)
    v2 = jnp.max(lg2, axis=-1, keepdims=True)
    i2 = jnp.min(jnp.where(lg2 == v2, lane, float(LANES)), axis=-1, keepdims=True)
    e2 = jnp.exp(v2 - v1)
    w1 = 1.0 / (1.0 + e2)
    w2 = e2 * w1
    info_ref[...] = jnp.where(lane == 0, i1, jnp.where(lane == 1, i2,
                              jnp.where(lane == 2, w1, jnp.where(lane == 3, w2, 0.0))))


def _router(h, mods, g, layer, w_router_pad, t_rows):
    tm = ROUTER_TM
    return pl.pallas_call(
        _router_body,
        grid=(t_rows // tm,),
        in_specs=[
            pl.BlockSpec((tm, D), lambda i: (i, 0)),
            _layer_vec_spec(layer),
            _mod_spec(layer, 3, tm),
            _mod_spec(layer, 4, tm),
            pl.BlockSpec((D, LANES), lambda i: (0, 0)),
        ],
        out_specs=[pl.BlockSpec((tm * SLABS, LANES), lambda i: (i, 0)),
                   pl.BlockSpec((tm, LANES), lambda i: (i, 0))],
        out_shape=[jax.ShapeDtypeStruct((t_rows * SLABS, LANES), F32),
                   jax.ShapeDtypeStruct((t_rows, LANES), F32)],
        compiler_params=_params("arbitrary"),
        name="router",
    )(h, g, mods, mods, w_router_pad)


def _row_gather_step(i, n_steps, idx_ref, idx_next_ref, src_ref, buf, sem, rows, consume):
    def row_copy(src_row, r, slot):
        return pltpu.make_async_copy(
            src_ref.at[pl.ds(pl.multiple_of(src_row * SLABS, SLABS), SLABS)],
            buf.at[slot, pl.ds(pl.multiple_of(r * ROW_PITCH, SUBLANES), SLABS)],
            sem.at[slot])

    def issue(ref, slot):
        def body(r2, carry):
            for k in range(2):
                r = 2 * r2 + k
                row_copy(ref[0, 0, r], r, slot).start(priority=k)
            return carry
        lax.fori_loop(0, rows // 2, body, 0, unroll=4)

    def step(slot):
        if slot == 0:
            @pl.when(i == 0)
            def _():
                issue(idx_ref, 0)

        @pl.when(i + 1 < n_steps)
        def _():
            issue(idx_next_ref, 1 - slot)

        pltpu.make_async_copy(src_ref.at[pl.ds(0, rows * SLABS)], buf.at[slot, pl.ds(0, rows * SLABS)],
                              sem.at[slot]).wait()
        consume(slot)

    @pl.when(i % 2 == 0)
    def _():
        step(0)

    @pl.when(i % 2 == 1)
    def _():
        step(1)


def _gather_body(idx_ref, idx_next_ref, src_ref, o_ref, buf, sem):
    def consume(slot):
        for s in range(SLABS):
            o_ref[:, s * LANES:(s + 1) * LANES] = buf[slot, pl.ds(s, GATHER_ROWS, stride=ROW_PITCH), :].astype(BF16)

    _row_gather_step(pl.program_id(0), pl.num_programs(0), idx_ref, idx_next_ref, src_ref, buf, sem,
                     GATHER_ROWS, consume)


def _gather_rows(src_slabs, idx):
    n = idx.shape[0]
    assert n % GATHER_ROWS == 0
    nsteps = n // GATHER_ROWS
    idx3 = idx.reshape(nsteps, 1, GATHER_ROWS)
    return pl.pallas_call(
        _gather_body,
        grid=(nsteps,),
        in_specs=[pl.BlockSpec((1, 1, GATHER_ROWS), lambda i: (i, 0, 0), memory_space=pltpu.SMEM),
                  pl.BlockSpec((1, 1, GATHER_ROWS), lambda i: (jnp.minimum(i + 1, nsteps - 1), 0, 0),
                               memory_space=pltpu.SMEM),
                  pl.BlockSpec(memory_space=pl.ANY)],
        out_specs=pl.BlockSpec((GATHER_ROWS, D), lambda i: (i, 0)),
        out_shape=jax.ShapeDtypeStruct((n, D), BF16),
        scratch_shapes=[pltpu.VMEM((2, GATHER_ROWS * ROW_PITCH, LANES), F32), pltpu.SemaphoreType.DMA((2,))],
        compiler_params=_params("arbitrary"),
        name="moe_gather",
    )(idx3, idx3, src_slabs)


def _moe_up_body(te_ref, nu_ref, x_ref, wg_ref, wu_ref, o_ref, wg_bf, wu_bf):
    i = pl.program_id(1)

    @pl.when(i < nu_ref[0])
    def _():
        @pl.when(jnp.logical_or(i == 0, te_ref[i] != te_ref[jnp.maximum(i - 1, 0)]))
        def _():
            wg_bf[...] = wg_ref[0, 0].astype(BF16)
            wu_bf[...] = wu_ref[0, 0].astype(BF16)

        x = x_ref[...]
        for c in range(BN_UP // MXU_COLS):
            cs = slice(c * MXU_COLS, (c + 1) * MXU_COLS)
            a = jnp.dot(x, wg_bf[:, cs], preferred_element_type=F32)
            b = jnp.dot(x, wu_bf[:, cs], preferred_element_type=F32)
            o_ref[:, cs] = ((a * _sigmoid(a)) * b).astype(BF16)


def _moe_down_body(te_ref, nu_ref, a_ref, wd_ref, o_ref):
    @pl.when(pl.program_id(0) < nu_ref[0])
    def _():
        a = a_ref[...]
        for c in range(D // MXU_COLS):
            y = jnp.dot(a, wd_ref[0, :, c * MXU_COLS:(c + 1) * MXU_COLS], preferred_element_type=F32)
            for k in range(MXU_COLS // LANES):
                s = c * (MXU_COLS // LANES) + k
                o_ref[pl.ds(s, TME_DOWN, stride=SLABS), :] = y[:, k * LANES:(k + 1) * LANES]


def _moe_experts(xs, w_gu_all, moe_idx, w_down, tile_expert, n_used):
    n_tiles = xs.shape[0] // TME_UP
    nj = DFF // BN_UP

    def tile(i, nu):
        return jnp.minimum(i, nu[0] - 1)

    act = pl.pallas_call(
        _moe_up_body,
        grid_spec=pltpu.PrefetchScalarGridSpec(
            num_scalar_prefetch=2,
            grid=(nj, n_tiles),
            in_specs=[
                pl.BlockSpec((TME_UP, D), lambda j, i, te, nu: (tile(i, nu), 0)),
                pl.BlockSpec((1, 1, D, BN_UP), lambda j, i, te, nu: (moe_idx, te[tile(i, nu)], 0, j)),
                pl.BlockSpec((1, 1, D, BN_UP), lambda j, i, te, nu: (moe_idx, te[tile(i, nu)], 0, nj + j)),
            ],
            out_specs=pl.BlockSpec((TME_UP, BN_UP), lambda j, i, te, nu: (tile(i, nu), j)),
            scratch_shapes=[pltpu.VMEM((D, BN_UP), BF16), pltpu.VMEM((D, BN_UP), BF16)],
        ),
        out_shape=jax.ShapeDtypeStruct((n_tiles * TME_UP, DFF), BF16),
        compiler_params=_params("arbitrary", "arbitrary"),
        name="moe_up",
    )(tile_expert, n_used, xs, w_gu_all, w_gu_all)

    split = TME_UP // TME_DOWN
    return pl.pallas_call(
        _moe_down_body,
        grid_spec=pltpu.PrefetchScalarGridSpec(
            num_scalar_prefetch=2,
            grid=(n_tiles * split,),
            in_specs=[
                pl.BlockSpec((TME_DOWN, DFF), lambda i, te, nu: (tile(i, nu), 0)),
                pl.BlockSpec((1, DFF, D), lambda i, te, nu: (te[tile(i, nu)], 0, 0),
                             pipeline_mode=pl.Buffered(1)),
            ],
            out_specs=pl.BlockSpec((TME_DOWN * SLABS, LANES), lambda i, te, nu: (tile(i, nu), 0)),
        ),
        out_shape=jax.ShapeDtypeStruct((n_tiles * TME_UP * SLABS, LANES), F32),
        compiler_params=_params("arbitrary"),
        name="moe_down",
    )(jnp.repeat(tile_expert, split), n_used * split, act, w_down)


def _routing_tables(info, t_rows, n_tiles):
    idx = info[:, :2].astype(jnp.int32)
    flat_e = idx.T.reshape(-1)
    onehot = (flat_e[:, None] == jnp.arange(E, dtype=jnp.int32)[None, :]).astype(jnp.int32)
    csum = jnp.cumsum(onehot, axis=0)
    rank = jnp.sum((csum - onehot) * onehot, axis=1)
    counts = csum[-1]
    padded = ((counts + TME - 1) // TME) * TME
    ends = jnp.cumsum(padded)
    starts = ends - padded
    pos = starts[flat_e] + rank
    n_rows = n_tiles * TME
    tile_start = jnp.arange(n_tiles, dtype=jnp.int32) * TME
    tile_expert = jnp.minimum(jnp.sum((tile_start[:, None] >= ends[None, :]).astype(jnp.int32), axis=1), E - 1)
    n_used = (ends[-1] // TME).astype(jnp.int32).reshape(1)
    token_of_pos = jnp.zeros((n_rows,), jnp.int32).at[pos].set(
        jnp.arange(2 * t_rows, dtype=jnp.int32) % t_rows)
    return pos.astype(jnp.int32), token_of_pos, tile_expert.astype(jnp.int32), n_used


def _combine_body(idx_ref, idx_next_ref, ys_ref, h_ref, info_ref, gate_ref, *rest, final):
    if final:
        fg_ref, o_ref, buf, sem = rest
    else:
        o_ref, buf, sem = rest

    def consume(slot):
        w0 = jnp.broadcast_to(info_ref[:, 2:3], (COMBINE_TM, LANES))
        w1 = jnp.broadcast_to(info_ref[:, 3:4], (COMBINE_TM, LANES))
        for s in range(SLABS):
            cols = slice(s * LANES, (s + 1) * LANES)
            y = (w0 * buf[slot, pl.ds(s, COMBINE_TM, stride=ROW_PITCH), :]
                 + w1 * buf[slot, pl.ds(COMBINE_TM * ROW_PITCH + s, COMBINE_TM, stride=ROW_PITCH), :])
            o_ref[:, cols] = h_ref[:, cols] + gate_ref[0, :, cols] * y
        if final:
            h = o_ref[...]
            o_ref[...] = (h * lax.rsqrt(jnp.mean(h * h, axis=-1, keepdims=True) + EPS)) * fg_ref[...]

    _row_gather_step(pl.program_id(0), pl.num_programs(0), idx_ref, idx_next_ref, ys_ref, buf, sem,
                     2 * COMBINE_TM, consume)


def _combine(h, ys_slabs, pos, info, mods, layer, t_rows, final_g):
    tm = COMBINE_TM
    nt = t_rows // tm
    final = final_g is not None
    idx3 = pos.reshape(2, nt, tm).transpose(1, 0, 2).reshape(nt, 1, 2 * tm)
    in_specs = [pl.BlockSpec((1, 1, 2 * tm), lambda i: (i, 0, 0), memory_space=pltpu.SMEM),
                pl.BlockSpec((1, 1, 2 * tm), lambda i: (jnp.minimum(i + 1, nt - 1), 0, 0),
                             memory_space=pltpu.SMEM),
                pl.BlockSpec(memory_space=pl.ANY),
                pl.BlockSpec((tm, D), lambda i: (i, 0)),
                pl.BlockSpec((tm, LANES), lambda i: (i, 0)),
                _mod_spec(layer, 5, tm)]
    args = [idx3, idx3, ys_slabs, h, info, mods]
    if final:
        in_specs.append(pl.BlockSpec((1, D), lambda i: (0, 0)))
        args.append(final_g.reshape(1, D))
    return pl.pallas_call(
        functools.partial(_combine_body, final=final),
        grid=(nt,),
        in_specs=in_specs,
        out_specs=pl.BlockSpec((tm, D), lambda i: (i, 0)),
        out_shape=jax.ShapeDtypeStruct((t_rows, D), F32),
        scratch_shapes=[pltpu.VMEM((2, 2 * tm * ROW_PITCH, LANES), F32), pltpu.SemaphoreType.DMA((2,))],
        compiler_params=_params("arbitrary"),
        name="moe_combine",
    )(*args)


def _rope_tables():
    rows = S // GRID_W
    row = jnp.repeat(jnp.arange(rows, dtype=F32), GRID_W)
    col = jnp.tile(jnp.arange(GRID_W, dtype=F32), rows)
    n_freq = HD // 4
    inv_freq = ROPE_BASE ** (-jnp.arange(n_freq, dtype=F32) / n_freq)
    ang = jnp.concatenate([row[:, None] * inv_freq, col[:, None] * inv_freq], axis=-1)
    ang = jnp.concatenate([ang, ang], axis=-1)
    sign = jnp.where(jnp.arange(HD) < HD // 2, -1.0, 1.0).astype(F32)
    cos = jnp.concatenate([jnp.cos(ang), jnp.ones((TM, HD), F32)], axis=0)
    sin = jnp.concatenate([jnp.sin(ang) * sign, jnp.zeros((TM, HD), F32)], axis=0)
    scale = HD ** -0.5 * math.log2(math.e)
    return jnp.stack([cos * scale, cos]), jnp.stack([sin * scale, sin])


def kernel(x, c, ctx, c_ctx, ada_w, ada_b, norm_mix_g, norm_ffn_g, attn_w_qkv, attn_lambda, attn_subln_g, attn_w_o, conv_w_in, conv_b_in, conv_w_dw, conv_b_dw, conv_ln_g, conv_ln_b, conv_w_out, conv_b_out, ffn_w_gu, ffn_w_down, moe_router, moe_w_gu, moe_w_down, final_g):
    x_lat, x_ctx = x.reshape(T_LAT, D), ctx.reshape(T_CTX, D)
    h = None
    cond =jnp.concatenate([c, c_ctx[None, :], jnp.zeros((COND_ROWS - B - 1, D), F32)], axis=0)
    mods = _ada_mods(cond, ada_w, ada_b)
    cos_t, sin_t = _rope_tables()
    mix_g = norm_mix_g.reshape(DEPTH, 1, D)
    ffn_g = norm_ffn_g.reshape(DEPTH, 1, D)

    for i in range(DEPTH):
        mix_idx = i // 2
        ffn_idx = i // 2
        ctx_live = i < LAST_CTX_READER
        t_rows = T_ALL if ctx_live else T_LAT
        if i % 2 == 0:
            lambda_init = 0.8 - 0.6 * math.exp(-0.3 * i)
            w_qkv = attn_w_qkv[mix_idx].astype(BF16)
            w_o = attn_w_o[mix_idx].astype(BF16)
            if h is None:
                qkv = _qkv_proj(x_lat, mods, mix_g, i, w_qkv, cos_t, sin_t)
                qkv = _qkv_proj(x_ctx, mods, mix_g, i, w_qkv, cos_t, sin_t, row_off=T_LAT, out=qkv)
                o = _attention(qkv, attn_lambda, attn_subln_g, mix_idx, lambda_init, ctx_live)
                h = _resid_proj(o, w_o, None, x_lat, mods, i, 2, t_rows, "attn_out", n_rows=T_LAT)
                h = _resid_proj(o, w_o, None, x_ctx, mods, i, 2, t_rows, "attn_out", n_rows=T_CTX,
                                row_off=T_LAT, out=h)
            else:
                qkv = _qkv_proj(h, mods, mix_g, i, w_qkv, cos_t, sin_t)
                o = _attention(qkv, attn_lambda, attn_subln_g, mix_idx, lambda_init, ctx_live)
                h = _resid_proj(o, w_o, None, h, mods, i, 2, t_rows, "attn_out")
        else:
            u = _glu_proj(h, mods, mix_g, i, conv_w_in[mix_idx].astype(BF16),
                          (mix_idx, conv_b_in.reshape(-1, 1, 2 * D)), t_rows, "glu", F32, "conv_in")
            z = _conv_module(u, conv_w_dw, conv_b_dw.reshape(-1, 1, D), conv_ln_g.reshape(-1, 1, D),
                             conv_ln_b.reshape(-1, 1, D), mix_idx, t_rows)
            h = _resid_proj(z, conv_w_out[mix_idx].astype(BF16), (mix_idx, conv_b_out.reshape(-1, 1, D)),
                            h, mods, i, 2, t_rows, "conv_out")
        if i % 2 == 0:
            act = _glu_proj(h, mods, ffn_g, i, ffn_w_gu[ffn_idx].astype(BF16), None, t_rows,
                            "swiglu", BF16, "ffn_up")
            h = _resid_proj(act, ffn_w_down[ffn_idx].astype(BF16), None, h, mods, i, 5, t_rows, "ffn_down")
        else:
            n_tiles = N_TILES_E if ctx_live else N_TILES_E_LAT
            w_r = jnp.zeros((D, LANES), F32).at[:, :E].set(moe_router[ffn_idx])
            hm_slabs, info = _router(h, mods, ffn_g, i, w_r, t_rows)
            pos, token_of_pos, tile_expert, n_used = _routing_tables(info, t_rows, n_tiles)
            xs = _gather_rows(hm_slabs, token_of_pos)
            ys_slabs = _moe_experts(xs, moe_w_gu, ffn_idx, moe_w_down[ffn_idx].astype(BF16),
                                    tile_expert, n_used)
            h = _combine(h, ys_slabs, pos, info, mods, i, t_rows, final_g if i == DEPTH - 1 else None)
    return h.reshape(B, S, D)
```

```python
import functools
import math

import jax
import jax.numpy as jnp
from jax import lax
from jax.experimental import pallas as pl
from jax.experimental.pallas import tpu as pltpu

F32 = jnp.float32
BF16 = jnp.bfloat16

D = 2048
B = 16
S = 2048
L = 256
DEPTH = 4
GRID_W = 64
H = 8
HD = 128
VD = 2 * HD
ROPE_BASE = 10000.0
CW = 31
CPAD = (CW - 1) // 2
DFF = 5632
E = 8
NMOD = 6
EPS = 1e-6

T_LAT = B * S
T_CTX = B * L
T_ALL = T_LAT + T_CTX
COND_ROWS = 24
CTX_COND_ROW = B
LAST_CTX_READER = 2

V7X_VMEM_LIMIT_BYTES = 56 * 1024 * 1024
LANES = 128
SUBLANES = 8
HALO = 16
SH_EXTRA = SUBLANES * ((CW - 1 + HALO - CPAD) // SUBLANES)

TM = 1024
BN_FFN_UP = 512
BN_CONV_IN = 1024
BN_QKV = 2048
TM_RESID = 512
TQ = 1024
TQ_SUB = 128
TS = 256
MXU_COLS = 256
TME = 1024
TME_UP = TME
TME_DOWN = 512
BN_UP = 512
SLABS = D // LANES
ROW_PITCH = SLABS + SUBLANES
ROUTER_TM = 512
GATHER_ROWS = 512
COMBINE_TM = 512
N_TILES_E = (2 * T_ALL + E * (TME - 1) + TME - 1) // TME
N_TILES_E_LAT = (2 * T_LAT + E * (TME - 1) + TME - 1) // TME


def _params(*sem):
    return pltpu.CompilerParams(dimension_semantics=sem, vmem_limit_bytes=V7X_VMEM_LIMIT_BYTES)


def _sigmoid(x):
    return 1.0 / (1.0 + jnp.exp(-x))


def _modulate(x, g, shift, scale):
    xn = x * lax.rsqrt(jnp.mean(x * x, axis=-1, keepdims=True) + EPS)
    return (xn * g) * (1.0 + scale) + shift


def _modulate_into(x_ref, g_ref, sh_ref, sc_ref, out_ref):
    g, shift, scale = g_ref[0], sh_ref[0], sc_ref[0]
    rb = 2 * SUBLANES

    def body(r, carry):
        rows = pl.ds(pl.multiple_of(r * rb, rb), rb)
        out_ref[rows, :] = _modulate(x_ref[rows, :], g, shift, scale).astype(out_ref.dtype)
        return carry

    lax.fori_loop(0, x_ref.shape[0] // rb, body, 0, unroll=8)


def _cond_row(i, tm):
    return jnp.where(i * tm < T_LAT, (i * tm) // S, CTX_COND_ROW)


def _mod_spec(layer, which, tm, tile_off=0):
    def imap(i, *rest):
        return ((layer * COND_ROWS + _cond_row(i + tile_off, tm)) * NMOD + which, 0, 0)
    return pl.BlockSpec((1, 1, D), imap)


def _layer_vec_spec(idx):
    return pl.BlockSpec((1, 1, D), lambda *grid: (idx, 0, 0))


def _ada_body(c_ref, w_ref, b_ref, o_ref):
    x = c_ref[...]
    s = (x * _sigmoid(x)).astype(BF16)
    o_ref[0] = jnp.dot(s, w_ref[0].astype(BF16), preferred_element_type=F32) + b_ref[0]


def _ada_mods(cond, ada_w, ada_b):
    bn = 1024
    n = NMOD * D
    out = pl.pallas_call(
        _ada_body,
        grid=(DEPTH, n // bn),
        in_specs=[
            pl.BlockSpec((COND_ROWS, D), lambda l, j: (0, 0)),
            pl.BlockSpec((1, D, bn), lambda l, j: (l, 0, j)),
            pl.BlockSpec((1, 1, bn), lambda l, j: (l, 0, j)),
        ],
        out_specs=pl.BlockSpec((1, COND_ROWS, bn), lambda l, j: (l, 0, j)),
        out_shape=jax.ShapeDtypeStruct((DEPTH, COND_ROWS, n), F32),
        compiler_params=_params("arbitrary", "arbitrary"),
        name="ada_mods",
    )(cond, ada_w, ada_b.reshape(DEPTH, 1, n))
    return out.reshape(DEPTH * COND_ROWS * NMOD, 1, D)


def _qkv_body(x_ref, g_ref, sh_ref, sc_ref, w_ref, cos_ref, sin_ref, *rest, n_rope):
    o_ref, hm_ref = rest[-2:]
    j = pl.program_id(1)

    @pl.when(j == 0)
    def _():
        _modulate_into(x_ref, g_ref, sh_ref, sc_ref, hm_ref)

    hm = hm_ref[...]

    def chunk(c):
        return jnp.dot(hm, w_ref[:, c * MXU_COLS:(c + 1) * MXU_COLS], preferred_element_type=F32)

    @pl.when(j < n_rope)
    def _():
        cos = cos_ref[0]
        sin = sin_ref[0]
        for c in range(BN_QKV // MXU_COLS):
            acc = chunk(c)
            for k in range(MXU_COLS // HD):
                xc = acc[:, k * HD:(k + 1) * HD]
                col = c * MXU_COLS + k * HD
                o_ref[:, col:col + HD] = (xc * cos + pltpu.roll(xc, HD // 2, 1) * sin).astype(BF16)

    @pl.when(j >= n_rope)
    def _():
        for c in range(BN_QKV // MXU_COLS):
            o_ref[:, c * MXU_COLS:(c + 1) * MXU_COLS] = chunk(c).astype(BF16)


def _qkv_proj(h, mods, g, layer, w_qkv, cos_t, sin_t, row_off=0, out=None):
    off = row_off // TM
    nq = D // BN_QKV
    tiles_per_seq = S // TM

    def rope_map(i, j):
        gi = i + off
        return (jnp.where(j < nq, 0, 1),
                jnp.where(gi * TM < T_LAT, gi % tiles_per_seq, tiles_per_seq), 0)

    in_specs = [
        pl.BlockSpec((TM, D), lambda i, j: (i, 0)),
        _layer_vec_spec(layer),
        _mod_spec(layer, 0, TM, off),
        _mod_spec(layer, 1, TM, off),
        pl.BlockSpec((D, BN_QKV), lambda i, j: (0, j)),
        pl.BlockSpec((1, TM, HD), rope_map),
        pl.BlockSpec((1, TM, HD), rope_map),
    ]
    args = [h, g, mods, mods, w_qkv, cos_t, sin_t]
    aliases = {}
    if out is not None:
        in_specs.append(pl.BlockSpec(memory_space=pl.ANY))
        args.append(out)
        aliases = {len(args) - 1: 0}
    return pl.pallas_call(
        functools.partial(_qkv_body, n_rope=2 * nq),
        grid=(h.shape[0] // TM, 3 * D // BN_QKV),
        in_specs=in_specs,
        out_specs=pl.BlockSpec((TM, BN_QKV), lambda i, j: (i + off, j)),
        out_shape=jax.ShapeDtypeStruct((T_ALL, 3 * D), BF16),
        scratch_shapes=[pltpu.VMEM((TM, D), BF16)],
        input_output_aliases=aliases,
        compiler_params=_params("arbitrary", "arbitrary"),
        name="qkv_proj",
    )(*args)


def _attn_body(lam_ref, g_ref, q_ref, kc_ref, vc_ref, *rest, lambda_init, with_lat):
    if with_lat:
        k_ref, v_ref, o_ref = rest
    else:
        o_ref = rest[-1]
    lp = lam_ref[0]
    lam = (jnp.exp(jnp.sum(lp[0:1] * lp[1:2], axis=-1, keepdims=True))
           - jnp.exp(jnp.sum(lp[2:3] * lp[3:4], axis=-1, keepdims=True)) + lambda_init)
    nt_dims = (((1,), (1,)), ((), ()))
    n_rows = q_ref.shape[0]
    sub = min(n_rows, TQ_SUB)
    def scores(t):
        out = []
        for c in range(2):
            sl = slice(c * HD, (c + 1) * HD)
            qc = q_ref[t * sub:(t + 1) * sub, sl]
            s_c = lax.dot_general(qc, kc_ref[:, sl], nt_dims, preferred_element_type=F32)
            s_l = lax.dot_general(qc, k_ref[:, sl], nt_dims, preferred_element_type=F32) if with_lat else None
            out.append((s_c, s_l))
        return out

    def weights(sc):
        parts = []
        for s_c, s_l in sc:
            m = jnp.max(s_c, axis=-1, keepdims=True)
            if with_lat:
                m = jnp.maximum(m, jnp.max(s_l, axis=-1, keepdims=True))
                p_l = jnp.exp2(s_l - m)
            p_c = jnp.exp2(s_c - m)
            den = jnp.sum(p_c, axis=-1, keepdims=True)
            if with_lat:
                den = den + jnp.sum(p_l, axis=-1, keepdims=True)
            parts.append((p_c, p_l if with_lat else None, den))
        (p1c, p1l, den1), (p2c, p2l, den2) = parts
        r = lam * den1 / den2
        a_c = (p1c - r * p2c).astype(BF16)
        a_l = (p1l - r * p2l).astype(BF16) if with_lat else None
        return a_c, a_l, den1

    def output(t, w):
        a_c, a_l, den1 = w
        o = jnp.dot(a_c, vc_ref[...], preferred_element_type=F32)
        if with_lat:
            o = o + jnp.dot(a_l, v_ref[...], preferred_element_type=F32)
        o = o / den1
        o = o * lax.rsqrt(jnp.mean(o * o, axis=-1, keepdims=True) + EPS) * g_ref[0] * (1.0 - lambda_init)
        o_ref[t * sub:(t + 1) * sub, :] = o.astype(BF16)

    n_sub = n_rows // sub
    nxt = scores(0)
    for t in range(n_sub):
        cur = nxt
        if t + 1 < n_sub:
            nxt = scores(t + 1)
        output(t, weights(cur))


def _attention(qkv, lam_p, subln_g, mix_idx, lambda_init, ctx_out):
    t_out = T_ALL if ctx_out else T_LAT
    nqt = S // TQ
    kcol, vcol = D // VD, 2 * D // VD
    ctx_blk = T_LAT // L
    common = [
        pl.BlockSpec((1, 4, HD), lambda b, h, t: (mix_idx, 0, 0)),
        pl.BlockSpec((1, 1, VD), lambda b, h, t: (mix_idx, 0, 0)),
    ]
    ctx_kv = [
        pl.BlockSpec((L, VD), lambda b, h, t: (ctx_blk + b, kcol + h)),
        pl.BlockSpec((L, VD), lambda b, h, t: (ctx_blk + b, vcol + h)),
    ]
    g3 = subln_g.reshape(-1, 1, VD)
    o = pl.pallas_call(
        functools.partial(_attn_body, lambda_init=lambda_init, with_lat=True),
        grid=(B, H, nqt),
        in_specs=common + [pl.BlockSpec((TQ, VD), lambda b, h, t: (b * nqt + t, h))] + ctx_kv + [
            pl.BlockSpec((S, VD), lambda b, h, t: (b, kcol + h)),
            pl.BlockSpec((S, VD), lambda b, h, t: (b, vcol + h)),
        ],
        out_specs=pl.BlockSpec((TQ, VD), lambda b, h, t: (b * nqt + t, h)),
        out_shape=jax.ShapeDtypeStruct((t_out, D), BF16),
        compiler_params=_params("arbitrary", "arbitrary", "arbitrary"),
        name="attn_lat",
    )(lam_p, g3, qkv, qkv, qkv, qkv, qkv)
    if not ctx_out:
        return o
    return pl.pallas_call(
        functools.partial(_attn_body, lambda_init=lambda_init, with_lat=False),
        grid=(B, H, 1),
        in_specs=common + [pl.BlockSpec((L, VD), lambda b, h, t: (ctx_blk + b, h))] + ctx_kv + [
            pl.BlockSpec(memory_space=pl.ANY),
        ],
        out_specs=pl.BlockSpec((L, VD), lambda b, h, t: (ctx_blk + b, h)),
        out_shape=jax.ShapeDtypeStruct((t_out, D), BF16),
        input_output_aliases={5: 0},
        compiler_params=_params("arbitrary", "arbitrary", "arbitrary"),
        name="attn_ctx",
    )(lam_p, g3, qkv, qkv, qkv, o)


def _resid_body(x_ref, w_ref, *rest, has_bias):
    o_ref = rest[-1]
    if has_bias:
        b_ref, r_ref, gate_ref = rest[:3]
    else:
        r_ref, gate_ref = rest[:2]
    x = x_ref[...]
    for c in range(D // MXU_COLS):
        cs = slice(c * MXU_COLS, (c + 1) * MXU_COLS)
        y = jnp.dot(x, w_ref[:, cs], preferred_element_type=F32)
        if has_bias:
            y = y + b_ref[0, :, cs]
        o_ref[:, cs] = r_ref[:, cs] + gate_ref[0, :, cs] * y


def _resid_proj(x, w, bias, resid, mods, layer, which, t_out, name, n_rows=None, row_off=0, out=None):
    k = x.shape[1]
    tm = TM_RESID if k > D else 2 * TM_RESID
    off = row_off // tm
    n_rows = t_out if n_rows is None else n_rows
    has_bias = bias is not None
    in_specs = [pl.BlockSpec((tm, k), lambda i: (i + off, 0)),
                pl.BlockSpec((k, D), lambda i: (0, 0), pipeline_mode=pl.Buffered(1))]
    args = [x, w]
    if has_bias:
        idx, arr = bias
        in_specs.append(_layer_vec_spec(idx))
        args.append(arr)
    in_specs += [pl.BlockSpec((tm, D), lambda i: (i, 0)), _mod_spec(layer, which, tm, off)]
    args += [resid, mods]
    aliases = {}
    if out is not None:
        in_specs.append(pl.BlockSpec(memory_space=pl.ANY))
        args.append(out)
        aliases = {len(args) - 1: 0}
    return pl.pallas_call(
        functools.partial(_resid_body, has_bias=has_bias),
        grid=(n_rows // tm,),
        in_specs=in_specs,
        out_specs=pl.BlockSpec((tm, D), lambda i: (i + off, 0)),
        out_shape=jax.ShapeDtypeStruct((t_out, D), F32),
        input_output_aliases=aliases,
        compiler_params=_params("arbitrary"),
        name=name,
    )(*args)


def _glu_body(x_ref, g_ref, sh_ref, sc_ref, wa_ref, wb_ref, *rest, kind):
    if kind == "glu":
        ba_ref, bb_ref, o_ref, hm_ref = rest
    else:
        o_ref, hm_ref = rest

    @pl.when(pl.program_id(1) == 0)
    def _():
        _modulate_into(x_ref, g_ref, sh_ref, sc_ref, hm_ref)

    hm = hm_ref[...]
    for c in range(wa_ref.shape[1] // MXU_COLS):
        cs = slice(c * MXU_COLS, (c + 1) * MXU_COLS)
        a = jnp.dot(hm, wa_ref[:, cs], preferred_element_type=F32)
        b = jnp.dot(hm, wb_ref[:, cs], preferred_element_type=F32)
        if kind == "glu":
            o = (a + ba_ref[0, :, cs]) * _sigmoid(b + bb_ref[0, :, cs])
        else:
            o = (a * _sigmoid(a)) * b
        o_ref[:, cs] = o.astype(o_ref.dtype)


def _glu_proj(h, mods, g, layer, w, bias, t_rows, kind, out_dtype, bn, name):
    nh = w.shape[1] // 2
    nj = nh // bn
    in_specs = [
        pl.BlockSpec((TM, D), lambda i, j: (i, 0)),
        _layer_vec_spec(layer),
        _mod_spec(layer, 3 if kind == "swiglu" else 0, TM),
        _mod_spec(layer, 4 if kind == "swiglu" else 1, TM),
        pl.BlockSpec((D, bn), lambda i, j: (0, j)),
        pl.BlockSpec((D, bn), lambda i, j: (0, nj + j)),
    ]
    args = [h, g, mods, mods, w, w]
    if kind == "glu":
        idx, arr = bias
        in_specs += [pl.BlockSpec((1, 1, bn), lambda i, j: (idx, 0, j)),
                     pl.BlockSpec((1, 1, bn), lambda i, j: (idx, 0, nj + j))]
        args += [arr, arr]
    return pl.pallas_call(
        functools.partial(_glu_body, kind=kind),
        grid=(t_rows // TM, nj),
        in_specs=in_specs,
        out_specs=pl.BlockSpec((TM, bn), lambda i, j: (i, j)),
        out_shape=jax.ShapeDtypeStruct((t_rows, nh), out_dtype),
        scratch_shapes=[pltpu.VMEM((TM, D), BF16)],
        compiler_params=_params("arbitrary", "arbitrary"),
        name=name,
    )(*args)


def _conv_body(prev_ref, cur_ref, next_ref, w_ref, b_ref, lg_ref, lb_ref, o_ref, ext_ref, y_ref, sh_ref):
    i = pl.program_id(0)
    tiles_per_seq = S // TS
    is_ctx = i * TS >= T_LAT
    first = jnp.logical_or(is_ctx, i % tiles_per_seq == 0)
    last = jnp.logical_or(is_ctx, i % tiles_per_seq == tiles_per_seq - 1)
    ext_ref[0:HALO, :] = jnp.where(first, 0.0, prev_ref[...])
    ext_ref[HALO:HALO + TS, :] = cur_ref[...]
    ext_ref[HALO + TS:2 * HALO + TS, :] = jnp.where(last, 0.0, next_ref[...])

    rc = 64
    off = HALO - CPAD

    def chan(c, carry):
        cs = pl.ds(pl.multiple_of(c * LANES, LANES), LANES)
        for d in range(1, SUBLANES):
            sh_ref[d - 1] = ext_ref[pl.ds(d, TS + SH_EXTRA), cs]
        bias = b_ref[0, :, cs]
        accs = [jnp.broadcast_to(bias, (rc, LANES)) for _ in range(TS // rc)]
        for k in range(CW):
            q, d = divmod(k + off, SUBLANES)
            wk = jnp.broadcast_to(w_ref[0, k:k + 1, cs], (rc, LANES))
            for r in range(TS // rc):
                rows = pl.ds(r * rc + q * SUBLANES, rc)
                tap = ext_ref[rows, cs] if d == 0 else sh_ref[d - 1, rows, :]
                accs[r] = accs[r] + wk * tap
        for r in range(TS // rc):
            y_ref[pl.ds(r * rc, rc), cs] = accs[r]
        return carry

    lax.fori_loop(0, D // LANES, chan, 0)

    y = y_ref[...]
    yc = y - jnp.mean(y, axis=-1, keepdims=True)
    var = jnp.mean(yc * yc, axis=-1, keepdims=True)
    z = yc * lax.rsqrt(var + EPS) * lg_ref[0] + lb_ref[0]
    o_ref[...] = (z * _sigmoid(z)).astype(BF16)


def _conv_module(u, w_dw, b_dw, ln_g, ln_b, idx, t_rows):
    assert L == TS and TS % HALO == 0 and HALO >= CPAD
    hb = TS // HALO
    n_halo_blocks = u.shape[0] // HALO
    return pl.pallas_call(
        _conv_body,
        grid=(t_rows // TS,),
        in_specs=[
            pl.BlockSpec((HALO, D), lambda i: (jnp.maximum(i * hb - 1, 0), 0)),
            pl.BlockSpec((TS, D), lambda i: (i, 0)),
            pl.BlockSpec((HALO, D), lambda i: (jnp.minimum((i + 1) * hb, n_halo_blocks - 1), 0)),
            pl.BlockSpec((1, CW, D), lambda i: (idx, 0, 0)),
            _layer_vec_spec(idx),
            _layer_vec_spec(idx),
            _layer_vec_spec(idx),
        ],
        out_specs=pl.BlockSpec((TS, D), lambda i: (i, 0)),
        out_shape=jax.ShapeDtypeStruct((t_rows, D), BF16),
        scratch_shapes=[pltpu.VMEM((TS + 2 * HALO, D), F32), pltpu.VMEM((TS, D), F32),
                        pltpu.VMEM((SUBLANES - 1, TS + SH_EXTRA, LANES), F32)],
        compiler_params=_params("arbitrary"),
        name="dwconv_ln_silu",
    )(u, u, u, w_dw, b_dw, ln_g, ln_b)


def _router_body(x_ref, g_ref, sh_ref, sc_ref, wr_ref, hm_ref, info_ref):
    hm = _modulate(x_ref[...], g_ref[0], sh_ref[0], sc_ref[0])
    for s in range(SLABS):
        hm_ref[pl.ds(s, ROUTER_TM, stride=SLABS), :] = hm[:, s * LANES:(s + 1) * LANES]
    logits = jnp.dot(hm, wr_ref[...], precision=lax.Precision.HIGHEST, preferred_element_type=F32)
    lane = lax.broadcasted_iota(jnp.int32, logits.shape, 1).astype(F32)
    ninf = -jnp.inf
    lg = jnp.where(lane < E, logits, ninf)
    v1 = jnp.max(lg, axis=-1, keepdims=True)
    i1 = jnp.min(jnp.where(lg == v1, lane, float(LANES)), axis=-1, keepdims=True)
    lg2 = jnp.where(lane == i1, ninf, lg)
    v2 = jnp.max(lg2, axis=-1, keepdims=True)
    i2 = jnp.min(jnp.where(lg2 == v2, lane, float(LANES)), axis=-1, keepdims=True)
    e2 = jnp.exp(v2 - v1)
    w1 = 1.0 / (1.0 + e2)
    w2 = e2 * w1
    info_ref[...] = jnp.where(lane == 0, i1, jnp.where(lane == 1, i2,
                              jnp.where(lane == 2, w1, jnp.where(lane == 3, w2, 0.0))))


def _router(h, mods, g, layer, w_router_pad, t_rows):
    tm = ROUTER_TM
    return pl.pallas_call(
        _router_body,
        grid=(t_rows // tm,),
        in_specs=[
            pl.BlockSpec((tm, D), lambda i: (i, 0)),
            _layer_vec_spec(layer),
            _mod_spec(layer, 3, tm),
            _mod_spec(layer, 4, tm),
            pl.BlockSpec((D, LANES), lambda i: (0, 0)),
        ],
        out_specs=[pl.BlockSpec((tm * SLABS, LANES), lambda i: (i, 0)),
                   pl.BlockSpec((tm, LANES), lambda i: (i, 0))],
        out_shape=[jax.ShapeDtypeStruct((t_rows * SLABS, LANES), F32),
                   jax.ShapeDtypeStruct((t_rows, LANES), F32)],
        compiler_params=_params("arbitrary"),
        name="router",
    )(h, g, mods, mods, w_router_pad)


def _row_gather_step(i, n_steps, idx_ref, idx_next_ref, src_ref, buf, sem, rows, consume):
    def row_copy(src_row, r, slot):
        return pltpu.make_async_copy(
            src_ref.at[pl.ds(pl.multiple_of(src_row * SLABS, SLABS), SLABS)],
            buf.at[slot, pl.ds(pl.multiple_of(r * ROW_PITCH, SUBLANES), SLABS)],
            sem.at[slot])

    def issue(ref, slot):
        def body(r2, carry):
            for k in range(2):
                r = 2 * r2 + k
                row_copy(ref[0, 0, r], r, slot).start(priority=k)
            return carry
        lax.fori_loop(0, rows // 2, body, 0, unroll=4)

    def step(slot):
        if slot == 0:
            @pl.when(i == 0)
            def _():
                issue(idx_ref, 0)

        @pl.when(i + 1 < n_steps)
        def _():
            issue(idx_next_ref, 1 - slot)

        pltpu.make_async_copy(src_ref.at[pl.ds(0, rows * SLABS)], buf.at[slot, pl.ds(0, rows * SLABS)],
                              sem.at[slot]).wait()
        consume(slot)

    @pl.when(i % 2 == 0)
    def _():
        step(0)

    @pl.when(i % 2 == 1)
    def _():
        step(1)


def _gather_body(idx_ref, idx_next_ref, src_ref, o_ref, buf, sem):
    def consume(slot):
        for s in range(SLABS):
            o_ref[:, s * LANES:(s + 1) * LANES] = buf[slot, pl.ds(s, GATHER_ROWS, stride=ROW_PITCH), :].astype(BF16)

    _row_gather_step(pl.program_id(0), pl.num_programs(0), idx_ref, idx_next_ref, src_ref, buf, sem,
                     GATHER_ROWS, consume)


def _gather_rows(src_slabs, idx):
    n = idx.shape[0]
    assert n % GATHER_ROWS == 0
    nsteps = n // GATHER_ROWS
    idx3 = idx.reshape(nsteps, 1, GATHER_ROWS)
    return pl.pallas_call(
        _gather_body,
        grid=(nsteps,),
        in_specs=[pl.BlockSpec((1, 1, GATHER_ROWS), lambda i: (i, 0, 0), memory_space=pltpu.SMEM),
                  pl.BlockSpec((1, 1, GATHER_ROWS), lambda i: (jnp.minimum(i + 1, nsteps - 1), 0, 0),
                               memory_space=pltpu.SMEM),
                  pl.BlockSpec(memory_space=pl.ANY)],
        out_specs=pl.BlockSpec((GATHER_ROWS, D), lambda i: (i, 0)),
        out_shape=jax.ShapeDtypeStruct((n, D), BF16),
        scratch_shapes=[pltpu.VMEM((2, GATHER_ROWS * ROW_PITCH, LANES), F32), pltpu.SemaphoreType.DMA((2,))],
        compiler_params=_params("arbitrary"),
        name="moe_gather",
    )(idx3, idx3, src_slabs)


def _moe_up_body(te_ref, nu_ref, x_ref, wg_ref, wu_ref, o_ref, wg_bf, wu_bf):
    i = pl.program_id(1)

    @pl.when(i < nu_ref[0])
    def _():
        @pl.when(jnp.logical_or(i == 0, te_ref[i] != te_ref[jnp.maximum(i - 1, 0)]))
        def _():
            wg_bf[...] = wg_ref[0, 0].astype(BF16)
            wu_bf[...] = wu_ref[0, 0].astype(BF16)

        x = x_ref[...]
        for c in range(BN_UP // MXU_COLS):
            cs = slice(c * MXU_COLS, (c + 1) * MXU_COLS)
            a = jnp.dot(x, wg_bf[:, cs], preferred_element_type=F32)
            b = jnp.dot(x, wu_bf[:, cs], preferred_element_type=F32)
            o_ref[:, cs] = ((a * _sigmoid(a)) * b).astype(BF16)


def _moe_down_body(te_ref, nu_ref, a_ref, wd_ref, o_ref):
    @pl.when(pl.program_id(0) < nu_ref[0])
    def _():
        a = a_ref[...]
        for c in range(D // MXU_COLS):
            y = jnp.dot(a, wd_ref[0, :, c * MXU_COLS:(c + 1) * MXU_COLS], preferred_element_type=F32)
            for k in range(MXU_COLS // LANES):
                s = c * (MXU_COLS // LANES) + k
                o_ref[pl.ds(s, TME_DOWN, stride=SLABS), :] = y[:, k * LANES:(k + 1) * LANES]


def _moe_experts(xs, w_gu_all, moe_idx, w_down, tile_expert, n_used):
    n_tiles = xs.shape[0] // TME_UP
    nj = DFF // BN_UP

    def tile(i, nu):
        return jnp.minimum(i, nu[0] - 1)

    act = pl.pallas_call(
        _moe_up_body,
        grid_spec=pltpu.PrefetchScalarGridSpec(
            num_scalar_prefetch=2,
            grid=(nj, n_tiles),
            in_specs=[
                pl.BlockSpec((TME_UP, D), lambda j, i, te, nu: (tile(i, nu), 0)),
                pl.BlockSpec((1, 1, D, BN_UP), lambda j, i, te, nu: (moe_idx, te[tile(i, nu)], 0, j)),
                pl.BlockSpec((1, 1, D, BN_UP), lambda j, i, te, nu: (moe_idx, te[tile(i, nu)], 0, nj + j)),
            ],
            out_specs=pl.BlockSpec((TME_UP, BN_UP), lambda j, i, te, nu: (tile(i, nu), j)),
            scratch_shapes=[pltpu.VMEM((D, BN_UP), BF16), pltpu.VMEM((D, BN_UP), BF16)],
        ),
        out_shape=jax.ShapeDtypeStruct((n_tiles * TME_UP, DFF), BF16),
        compiler_params=_params("arbitrary", "arbitrary"),
        name="moe_up",
    )(tile_expert, n_used, xs, w_gu_all, w_gu_all)

    split = TME_UP // TME_DOWN
    return pl.pallas_call(
        _moe_down_body,
        grid_spec=pltpu.PrefetchScalarGridSpec(
            num_scalar_prefetch=2,
            grid=(n_tiles * split,),
            in_specs=[
                pl.BlockSpec((TME_DOWN, DFF), lambda i, te, nu: (tile(i, nu), 0)),
                pl.BlockSpec((1, DFF, D), lambda i, te, nu: (te[tile(i, nu)], 0, 0),
                             pipeline_mode=pl.Buffered(1)),
            ],
            out_specs=pl.BlockSpec((TME_DOWN * SLABS, LANES), lambda i, te, nu: (tile(i, nu), 0)),
        ),
        out_shape=jax.ShapeDtypeStruct((n_tiles * TME_UP * SLABS, LANES), F32),
        compiler_params=_params("arbitrary"),
        name="moe_down",
    )(jnp.repeat(tile_expert, split), n_used * split, act, w_down)


def _routing_tables(info, t_rows, n_tiles):
    idx = info[:, :2].astype(jnp.int32)
    flat_e = idx.T.reshape(-1)
    onehot = (flat_e[:, None] == jnp.arange(E, dtype=jnp.int32)[None, :]).astype(jnp.int32)
    csum = jnp.cumsum(onehot, axis=0)
    rank = jnp.sum((csum - onehot) * onehot, axis=1)
    counts = csum[-1]
    padded = ((counts + TME - 1) // TME) * TME
    ends = jnp.cumsum(padded)
    starts = ends - padded
    pos = starts[flat_e] + rank
    n_rows = n_tiles * TME
    tile_start = jnp.arange(n_tiles, dtype=jnp.int32) * TME
    tile_expert = jnp.minimum(jnp.sum((tile_start[:, None] >= ends[None, :]).astype(jnp.int32), axis=1), E - 1)
    n_used = (ends[-1] // TME).astype(jnp.int32).reshape(1)
    token_of_pos = jnp.zeros((n_rows,), jnp.int32).at[pos].set(
        jnp.arange(2 * t_rows, dtype=jnp.int32) % t_rows)
    return pos.astype(jnp.int32), token_of_pos, tile_expert.astype(jnp.int32), n_used


def _combine_body(idx_ref, idx_next_ref, ys_ref, h_ref, info_ref, gate_ref, *rest, final):
    if final:
        fg_ref, o_ref, buf, sem = rest
    else:
        o_ref, buf, sem = rest

    def consume(slot):
        w0 = jnp.broadcast_to(info_ref[:, 2:3], (COMBINE_TM, LANES))
        w1 = jnp.broadcast_to(info_ref[:, 3:4], (COMBINE_TM, LANES))
        for s in range(SLABS):
            cols = slice(s * LANES, (s + 1) * LANES)
            y = (w0 * buf[slot, pl.ds(s, COMBINE_TM, stride=ROW_PITCH), :]
                 + w1 * buf[slot, pl.ds(COMBINE_TM * ROW_PITCH + s, COMBINE_TM, stride=ROW_PITCH), :])
            o_ref[:, cols] = h_ref[:, cols] + gate_ref[0, :, cols] * y
        if final:
            h = o_ref[...]
            o_ref[...] = (h * lax.rsqrt(jnp.mean(h * h, axis=-1, keepdims=True) + EPS)) * fg_ref[...]

    _row_gather_step(pl.program_id(0), pl.num_programs(0), idx_ref, idx_next_ref, ys_ref, buf, sem,
                     2 * COMBINE_TM, consume)


def _combine(h, ys_slabs, pos, info, mods, layer, t_rows, final_g):
    tm = COMBINE_TM
    nt = t_rows // tm
    final = final_g is not None
    idx3 = pos.reshape(2, nt, tm).transpose(1, 0, 2).reshape(nt, 1, 2 * tm)
    in_specs = [pl.BlockSpec((1, 1, 2 * tm), lambda i: (i, 0, 0), memory_space=pltpu.SMEM),
                pl.BlockSpec((1, 1, 2 * tm), lambda i: (jnp.minimum(i + 1, nt - 1), 0, 0),
                             memory_space=pltpu.SMEM),
                pl.BlockSpec(memory_space=pl.ANY),
                pl.BlockSpec((tm, D), lambda i: (i, 0)),
                pl.BlockSpec((tm, LANES), lambda i: (i, 0)),
                _mod_spec(layer, 5, tm)]
    args = [idx3, idx3, ys_slabs, h, info, mods]
    if final:
        in_specs.append(pl.BlockSpec((1, D), lambda i: (0, 0)))
        args.append(final_g.reshape(1, D))
    return pl.pallas_call(
        functools.partial(_combine_body, final=final),
        grid=(nt,),
        in_specs=in_specs,
        out_specs=pl.BlockSpec((tm, D), lambda i: (i, 0)),
        out_shape=jax.ShapeDtypeStruct((t_rows, D), F32),
        scratch_shapes=[pltpu.VMEM((2, 2 * tm * ROW_PITCH, LANES), F32), pltpu.SemaphoreType.DMA((2,))],
        compiler_params=_params("arbitrary"),
        name="moe_combine",
    )(*args)


def _rope_tables():
    rows = S // GRID_W
    row = jnp.repeat(jnp.arange(rows, dtype=F32), GRID_W)
    col = jnp.tile(jnp.arange(GRID_W, dtype=F32), rows)
    n_freq = HD // 4
    inv_freq = ROPE_BASE ** (-jnp.arange(n_freq, dtype=F32) / n_freq)
    ang = jnp.concatenate([row[:, None] * inv_freq, col[:, None] * inv_freq], axis=-1)
    ang = jnp.concatenate([ang, ang], axis=-1)
    sign = jnp.where(jnp.arange(HD) < HD // 2, -1.0, 1.0).astype(F32)
    cos = jnp.concatenate([jnp.cos(ang), jnp.ones((TM, HD), F32)], axis=0)
    sin = jnp.concatenate([jnp.sin(ang) * sign, jnp.zeros((TM, HD), F32)], axis=0)
    scale = HD ** -0.5 * math.log2(math.e)
    return jnp.stack([cos * scale, cos]), jnp.stack([sin * scale, sin])


def kernel(x, c, ctx, c_ctx, ada_w, ada_b, norm_mix_g, norm_ffn_g, attn_w_qkv, attn_lambda, attn_subln_g, attn_w_o, conv_w_in, conv_b_in, conv_w_dw, conv_b_dw, conv_ln_g, conv_ln_b, conv_w_out, conv_b_out, ffn_w_gu, ffn_w_down, moe_router, moe_w_gu, moe_w_down, final_g):
    x_lat, x_ctx = x.reshape(T_LAT, D), ctx.reshape(T_CTX, D)
    h = None
    cond =jnp.concatenate([c, c_ctx[None, :], jnp.zeros((COND_ROWS - B - 1, D), F32)], axis=0)
    mods = _ada_mods(cond, ada_w, ada_b)
    cos_t, sin_t = _rope_tables()
    mix_g = norm_mix_g.reshape(DEPTH, 1, D)
    ffn_g = norm_ffn_g.reshape(DEPTH, 1, D)

    for i in range(DEPTH):
        mix_idx = i // 2
        ffn_idx = i // 2
        ctx_live = i < LAST_CTX_READER
        t_rows = T_ALL if ctx_live else T_LAT
        if i % 2 == 0:
            lambda_init = 0.8 - 0.6 * math.exp(-0.3 * i)
            w_qkv = attn_w_qkv[mix_idx].astype(BF16)
            w_o = attn_w_o[mix_idx].astype(BF16)
            if h is None:
                qkv = _qkv_proj(x_lat, mods, mix_g, i, w_qkv, cos_t, sin_t)
                qkv = _qkv_proj(x_ctx, mods, mix_g, i, w_qkv, cos_t, sin_t, row_off=T_LAT, out=qkv)
                o = _attention(qkv, attn_lambda, attn_subln_g, mix_idx, lambda_init, ctx_live)
                h = _resid_proj(o, w_o, None, x_lat, mods, i, 2, t_rows, "attn_out", n_rows=T_LAT)
                h = _resid_proj(o, w_o, None, x_ctx, mods, i, 2, t_rows, "attn_out", n_rows=T_CTX,
                                row_off=T_LAT, out=h)
            else:
                qkv = _qkv_proj(h, mods, mix_g, i, w_qkv, cos_t, sin_t)
                o = _attention(qkv, attn_lambda, attn_subln_g, mix_idx, lambda_init, ctx_live)
                h = _resid_proj(o, w_o, None, h, mods, i, 2, t_rows, "attn_out")
        else:
            u = _glu_proj(h, mods, mix_g, i, conv_w_in[mix_idx].astype(BF16),
                          (mix_idx, conv_b_in.reshape(-1, 1, 2 * D)), t_rows, "glu", F32, BN_CONV_IN, "conv_in")
            z = _conv_module(u, conv_w_dw, conv_b_dw.reshape(-1, 1, D), conv_ln_g.reshape(-1, 1, D),
                             conv_ln_b.reshape(-1, 1, D), mix_idx, t_rows)
            h = _resid_proj(z, conv_w_out[mix_idx].astype(BF16), (mix_idx, conv_b_out.reshape(-1, 1, D)),
                            h, mods, i, 2, t_rows, "conv_out")
        if i % 2 == 0:
            act = _glu_proj(h, mods, ffn_g, i, ffn_w_gu[ffn_idx].astype(BF16), None, t_rows,
                            "swiglu", BF16, BN_FFN_UP, "ffn_up")
            h = _resid_proj(act, ffn_w_down[ffn_idx].astype(BF16), None, h, mods, i, 5, t_rows, "ffn_down")
        else:
            n_tiles = N_TILES_E if ctx_live else N_TILES_E_LAT
            w_r = jnp.zeros((D, LANES), F32).at[:, :E].set(moe_router[ffn_idx])
            hm_slabs, info = _router(h, mods, ffn_g, i, w_r, t_rows)
            pos, token_of_pos, tile_expert, n_used = _routing_tables(info, t_rows, n_tiles)
            xs = _gather_rows(hm_slabs, token_of_pos)
            ys_slabs = _moe_experts(xs, moe_w_gu, ffn_idx, moe_w_down[ffn_idx].astype(BF16),
                                    tile_expert, n_used)
            h = _combine(h, ys_slabs, pos, info, mods, i, t_rows, final_g if i == DEPTH - 1 else None)
    return h.reshape(B, S, D)
```

```python
import functools
import math

import jax
import jax.numpy as jnp
from jax import lax
from jax.experimental import pallas as pl
from jax.experimental.pallas import tpu as pltpu

F32 = jnp.float32
BF16 = jnp.bfloat16

D = 2048
B = 16
S = 2048
L = 256
DEPTH = 4
GRID_W = 64
H = 8
HD = 128
VD = 2 * HD
ROPE_BASE = 10000.0
CW = 31
CPAD = (CW - 1) // 2
DFF = 5632
E = 8
NMOD = 6
EPS = 1e-6

T_LAT = B * S
T_CTX = B * L
T_ALL = T_LAT + T_CTX
COND_ROWS = 24
CTX_COND_ROW = B
LAST_CTX_READER = 2

V7X_VMEM_LIMIT_BYTES = 56 * 1024 * 1024
LANES = 128
SUBLANES = 8
HALO = 16
SH_EXTRA = SUBLANES * ((CW - 1 + HALO - CPAD) // SUBLANES)

TM = 1024
TM_FFN_UP = 2048
BN_FFN_UP = 512
BN_CONV_IN = 1024
BN_QKV = 2048
TM_RESID = 512
TQ = 1024
TQ_SUB = 128
TS = 256
MXU_COLS = 256
TME = 1024
TME_UP = TME
TME_DOWN = 512
BN_UP = 512
SLABS = D // LANES
ROW_PITCH = SLABS + SUBLANES
ROUTER_TM = 512
GATHER_ROWS = 512
COMBINE_TM = 256
N_TILES_E = (2 * T_ALL + E * (TME - 1) + TME - 1) // TME
N_TILES_E_LAT = (2 * T_LAT + E * (TME - 1) + TME - 1) // TME


def _params(*sem):
    return pltpu.CompilerParams(dimension_semantics=sem, vmem_limit_bytes=V7X_VMEM_LIMIT_BYTES)


def _sigmoid(x):
    return 1.0 / (1.0 + jnp.exp(-x))


def _modulate(x, g, shift, scale):
    xn = x * lax.rsqrt(jnp.mean(x * x, axis=-1, keepdims=True) + EPS)
    return (xn * g) * (1.0 + scale) + shift


def _modulate_into(x_ref, g_ref, sh_ref, sc_ref, out_ref):
    g, shift, scale = g_ref[0], sh_ref[0], sc_ref[0]
    rb = 2 * SUBLANES

    def body(r, carry):
        rows = pl.ds(pl.multiple_of(r * rb, rb), rb)
        out_ref[rows, :] = _modulate(x_ref[rows, :], g, shift, scale).astype(out_ref.dtype)
        return carry

    lax.fori_loop(0, x_ref.shape[0] // rb, body, 0, unroll=8)


def _cond_row(i, tm):
    return jnp.where(i * tm < T_LAT, (i * tm) // S, CTX_COND_ROW)


def _mod_spec(layer, which, tm, tile_off=0):
    def imap(i, *rest):
        return ((layer * COND_ROWS + _cond_row(i + tile_off, tm)) * NMOD + which, 0, 0)
    return pl.BlockSpec((1, 1, D), imap)


def _layer_vec_spec(idx):
    return pl.BlockSpec((1, 1, D), lambda *grid: (idx, 0, 0))


def _ada_body(c_ref, w_ref, b_ref, o_ref):
    x = c_ref[...]
    s = (x * _sigmoid(x)).astype(BF16)
    o_ref[0] = jnp.dot(s, w_ref[0].astype(BF16), preferred_element_type=F32) + b_ref[0]


def _ada_mods(cond, ada_w, ada_b):
    bn = 1024
    n = NMOD * D
    out = pl.pallas_call(
        _ada_body,
        grid=(DEPTH, n // bn),
        in_specs=[
            pl.BlockSpec((COND_ROWS, D), lambda l, j: (0, 0)),
            pl.BlockSpec((1, D, bn), lambda l, j: (l, 0, j)),
            pl.BlockSpec((1, 1, bn), lambda l, j: (l, 0, j)),
        ],
        out_specs=pl.BlockSpec((1, COND_ROWS, bn), lambda l, j: (l, 0, j)),
        out_shape=jax.ShapeDtypeStruct((DEPTH, COND_ROWS, n), F32),
        compiler_params=_params("arbitrary", "arbitrary"),
        name="ada_mods",
    )(cond, ada_w, ada_b.reshape(DEPTH, 1, n))
    return out.reshape(DEPTH * COND_ROWS * NMOD, 1, D)


def _qkv_body(x_ref, g_ref, sh_ref, sc_ref, w_ref, cos_ref, sin_ref, *rest, n_rope):
    o_ref, hm_ref = rest[-2:]
    j = pl.program_id(1)

    @pl.when(j == 0)
    def _():
        _modulate_into(x_ref, g_ref, sh_ref, sc_ref, hm_ref)

    hm = hm_ref[...]

    def chunk(c):
        return jnp.dot(hm, w_ref[:, c * MXU_COLS:(c + 1) * MXU_COLS], preferred_element_type=F32)

    @pl.when(j < n_rope)
    def _():
        cos = cos_ref[0]
        sin = sin_ref[0]
        for c in range(BN_QKV // MXU_COLS):
            acc = chunk(c)
            for k in range(MXU_COLS // HD):
                xc = acc[:, k * HD:(k + 1) * HD]
                col = c * MXU_COLS + k * HD
                o_ref[:, col:col + HD] = (xc * cos + pltpu.roll(xc, HD // 2, 1) * sin).astype(BF16)

    @pl.when(j >= n_rope)
    def _():
        for c in range(BN_QKV // MXU_COLS):
            o_ref[:, c * MXU_COLS:(c + 1) * MXU_COLS] = chunk(c).astype(BF16)


def _qkv_proj(h, mods, g, layer, w_qkv, cos_t, sin_t, row_off=0, out=None):
    off = row_off // TM
    nq = D // BN_QKV
    tiles_per_seq = S // TM

    def rope_map(i, j):
        gi = i + off
        return (jnp.where(j < nq, 0, 1),
                jnp.where(gi * TM < T_LAT, gi % tiles_per_seq, tiles_per_seq), 0)

    in_specs = [
        pl.BlockSpec((TM, D), lambda i, j: (i, 0)),
        _layer_vec_spec(layer),
        _mod_spec(layer, 0, TM, off),
        _mod_spec(layer, 1, TM, off),
        pl.BlockSpec((D, BN_QKV), lambda i, j: (0, j)),
        pl.BlockSpec((1, TM, HD), rope_map),
        pl.BlockSpec((1, TM, HD), rope_map),
    ]
    args = [h, g, mods, mods, w_qkv, cos_t, sin_t]
    aliases = {}
    if out is not None:
        in_specs.append(pl.BlockSpec(memory_space=pl.ANY))
        args.append(out)
        aliases = {len(args) - 1: 0}
    return pl.pallas_call(
        functools.partial(_qkv_body, n_rope=2 * nq),
        grid=(h.shape[0] // TM, 3 * D // BN_QKV),
        in_specs=in_specs,
        out_specs=pl.BlockSpec((TM, BN_QKV), lambda i, j: (i + off, j)),
        out_shape=jax.ShapeDtypeStruct((T_ALL, 3 * D), BF16),
        scratch_shapes=[pltpu.VMEM((TM, D), BF16)],
        input_output_aliases=aliases,
        compiler_params=_params("arbitrary", "arbitrary"),
        name="qkv_proj",
    )(*args)


def _attn_body(lam_ref, g_ref, q_ref, kc_ref, vc_ref, *rest, lambda_init, with_lat):
    if with_lat:
        k_ref, v_ref, o_ref = rest
    else:
        o_ref = rest[-1]
    lp = lam_ref[0]
    lam = (jnp.exp(jnp.sum(lp[0:1] * lp[1:2], axis=-1, keepdims=True))
           - jnp.exp(jnp.sum(lp[2:3] * lp[3:4], axis=-1, keepdims=True)) + lambda_init)
    nt_dims = (((1,), (1,)), ((), ()))
    n_rows = q_ref.shape[0]
    sub = min(n_rows, TQ_SUB)
    def scores(t):
        out = []
        for c in range(2):
            sl = slice(c * HD, (c + 1) * HD)
            qc = q_ref[t * sub:(t + 1) * sub, sl]
            s_c = lax.dot_general(qc, kc_ref[:, sl], nt_dims, preferred_element_type=F32)
            s_l = lax.dot_general(qc, k_ref[:, sl], nt_dims, preferred_element_type=F32) if with_lat else None
            out.append((s_c, s_l))
        return out

    def weights(sc):
        parts = []
        for s_c, s_l in sc:
            m = jnp.max(s_c, axis=-1, keepdims=True)
            if with_lat:
                m = jnp.maximum(m, jnp.max(s_l, axis=-1, keepdims=True))
                p_l = jnp.exp2(s_l - m)
            p_c = jnp.exp2(s_c - m)
            den = jnp.sum(p_c, axis=-1, keepdims=True)
            if with_lat:
                den = den + jnp.sum(p_l, axis=-1, keepdims=True)
            parts.append((p_c, p_l if with_lat else None, den))
        (p1c, p1l, den1), (p2c, p2l, den2) = parts
        r = lam * den1 / den2
        a_c = (p1c - r * p2c).astype(BF16)
        a_l = (p1l - r * p2l).astype(BF16) if with_lat else None
        return a_c, a_l, den1

    def output(t, w):
        a_c, a_l, den1 = w
        o = jnp.dot(a_c, vc_ref[...], preferred_element_type=F32)
        if with_lat:
            o = o + jnp.dot(a_l, v_ref[...], preferred_element_type=F32)
        o = o / den1
        o = o * lax.rsqrt(jnp.mean(o * o, axis=-1, keepdims=True) + EPS) * g_ref[0] * (1.0 - lambda_init)
        o_ref[t * sub:(t + 1) * sub, :] = o.astype(BF16)

    n_sub = n_rows // sub
    nxt = scores(0)
    for t in range(n_sub):
        cur = nxt
        if t + 1 < n_sub:
            nxt = scores(t + 1)
        output(t, weights(cur))


def _attention(qkv, lam_p, subln_g, mix_idx, lambda_init, ctx_out):
    t_out = T_ALL if ctx_out else T_LAT
    nqt = S // TQ
    kcol, vcol = D // VD, 2 * D // VD
    ctx_blk = T_LAT // L
    common = [
        pl.BlockSpec((1, 4, HD), lambda b, h, t: (mix_idx, 0, 0)),
        pl.BlockSpec((1, 1, VD), lambda b, h, t: (mix_idx, 0, 0)),
    ]
    ctx_kv = [
        pl.BlockSpec((L, VD), lambda b, h, t: (ctx_blk + b, kcol + h)),
        pl.BlockSpec((L, VD), lambda b, h, t: (ctx_blk + b, vcol + h)),
    ]
    g3 = subln_g.reshape(-1, 1, VD)
    o = pl.pallas_call(
        functools.partial(_attn_body, lambda_init=lambda_init, with_lat=True),
        grid=(B, H, nqt),
        in_specs=common + [pl.BlockSpec((TQ, VD), lambda b, h, t: (b * nqt + t, h))] + ctx_kv + [
            pl.BlockSpec((S, VD), lambda b, h, t: (b, kcol + h)),
            pl.BlockSpec((S, VD), lambda b, h, t: (b, vcol + h)),
        ],
        out_specs=pl.BlockSpec((TQ, VD), lambda b, h, t: (b * nqt + t, h)),
        out_shape=jax.ShapeDtypeStruct((t_out, D), BF16),
        compiler_params=_params("arbitrary", "arbitrary", "arbitrary"),
        name="attn_lat",
    )(lam_p, g3, qkv, qkv, qkv, qkv, qkv)
    if not ctx_out:
        return o
    return pl.pallas_call(
        functools.partial(_attn_body, lambda_init=lambda_init, with_lat=False),
        grid=(B, H, 1),
        in_specs=common + [pl.BlockSpec((L, VD), lambda b, h, t: (ctx_blk + b, h))] + ctx_kv + [
            pl.BlockSpec(memory_space=pl.ANY),
        ],
        out_specs=pl.BlockSpec((L, VD), lambda b, h, t: (ctx_blk + b, h)),
        out_shape=jax.ShapeDtypeStruct((t_out, D), BF16),
        input_output_aliases={5: 0},
        compiler_params=_params("arbitrary", "arbitrary", "arbitrary"),
        name="attn_ctx",
    )(lam_p, g3, qkv, qkv, qkv, o)


def _resid_body(x_ref, w_ref, *rest, has_bias):
    o_ref = rest[-1]
    if has_bias:
        b_ref, r_ref, gate_ref = rest[:3]
    else:
        r_ref, gate_ref = rest[:2]
    x = x_ref[...]
    for c in range(D // MXU_COLS):
        cs = slice(c * MXU_COLS, (c + 1) * MXU_COLS)
        y = jnp.dot(x, w_ref[:, cs], preferred_element_type=F32)
        if has_bias:
            y = y + b_ref[0, :, cs]
        o_ref[:, cs] = r_ref[:, cs] + gate_ref[0, :, cs] * y


def _resid_proj(x, w, bias, resid, mods, layer, which, t_out, name, n_rows=None, row_off=0, out=None):
    k = x.shape[1]
    tm = TM_RESID if k > D else 2 * TM_RESID
    off = row_off // tm
    n_rows = t_out if n_rows is None else n_rows
    has_bias = bias is not None
    in_specs = [pl.BlockSpec((tm, k), lambda i: (i + off, 0)),
                pl.BlockSpec((k, D), lambda i: (0, 0), pipeline_mode=pl.Buffered(1))]
    args = [x, w]
    if has_bias:
        idx, arr = bias
        in_specs.append(_layer_vec_spec(idx))
        args.append(arr)
    in_specs += [pl.BlockSpec((tm, D), lambda i: (i, 0)), _mod_spec(layer, which, tm, off)]
    args += [resid, mods]
    aliases = {}
    if out is not None:
        in_specs.append(pl.BlockSpec(memory_space=pl.ANY))
        args.append(out)
        aliases = {len(args) - 1: 0}
    return pl.pallas_call(
        functools.partial(_resid_body, has_bias=has_bias),
        grid=(n_rows // tm,),
        in_specs=in_specs,
        out_specs=pl.BlockSpec((tm, D), lambda i: (i + off, 0)),
        out_shape=jax.ShapeDtypeStruct((t_out, D), F32),
        input_output_aliases=aliases,
        compiler_params=_params("arbitrary"),
        name=name,
    )(*args)


def _glu_body(x_ref, g_ref, sh_ref, sc_ref, wa_ref, wb_ref, *rest, kind):
    if kind == "glu":
        ba_ref, bb_ref, o_ref, hm_ref = rest
    else:
        o_ref, hm_ref = rest

    @pl.when(pl.program_id(1) == 0)
    def _():
        _modulate_into(x_ref, g_ref, sh_ref, sc_ref, hm_ref)

    hm = hm_ref[...]
    for c in range(wa_ref.shape[1] // MXU_COLS):
        cs = slice(c * MXU_COLS, (c + 1) * MXU_COLS)
        a = jnp.dot(hm, wa_ref[:, cs], preferred_element_type=F32)
        b = jnp.dot(hm, wb_ref[:, cs], preferred_element_type=F32)
        if kind == "glu":
            o = (a + ba_ref[0, :, cs]) * _sigmoid(b + bb_ref[0, :, cs])
        else:
            o = (a * _sigmoid(a)) * b
        o_ref[:, cs] = o.astype(o_ref.dtype)


def _glu_proj(h, mods, g, layer, w, bias, t_rows, kind, out_dtype, tm, bn, name):
    nh = w.shape[1] // 2
    nj = nh // bn
    x_mode = {} if tm <= TM else dict(pipeline_mode=pl.Buffered(1))
    in_specs = [
        pl.BlockSpec((tm, D), lambda i, j: (i, 0), **x_mode),
        _layer_vec_spec(layer),
        _mod_spec(layer, 3 if kind == "swiglu" else 0, tm),
        _mod_spec(layer, 4 if kind == "swiglu" else 1, tm),
        pl.BlockSpec((D, bn), lambda i, j: (0, j)),
        pl.BlockSpec((D, bn), lambda i, j: (0, nj + j)),
    ]
    args = [h, g, mods, mods, w, w]
    if kind == "glu":
        idx, arr = bias
        in_specs += [pl.BlockSpec((1, 1, bn), lambda i, j: (idx, 0, j)),
                     pl.BlockSpec((1, 1, bn), lambda i, j: (idx, 0, nj + j))]
        args += [arr, arr]
    return pl.pallas_call(
        functools.partial(_glu_body, kind=kind),
        grid=(t_rows // tm, nj),
        in_specs=in_specs,
        out_specs=pl.BlockSpec((tm, bn), lambda i, j: (i, j)),
        out_shape=jax.ShapeDtypeStruct((t_rows, nh), out_dtype),
        scratch_shapes=[pltpu.VMEM((tm, D), BF16)],
        compiler_params=_params("arbitrary", "arbitrary"),
        name=name,
    )(*args)


def _conv_body(prev_ref, cur_ref, next_ref, w_ref, b_ref, lg_ref, lb_ref, o_ref, ext_ref, y_ref, sh_ref):
    i = pl.program_id(0)
    tiles_per_seq = S // TS
    is_ctx = i * TS >= T_LAT
    first = jnp.logical_or(is_ctx, i % tiles_per_seq == 0)
    last = jnp.logical_or(is_ctx, i % tiles_per_seq == tiles_per_seq - 1)
    ext_ref[0:HALO, :] = jnp.where(first, 0.0, prev_ref[...])
    ext_ref[HALO:HALO + TS, :] = cur_ref[...]
    ext_ref[HALO + TS:2 * HALO + TS, :] = jnp.where(last, 0.0, next_ref[...])

    rc = 64
    off = HALO - CPAD

    def chan(c, carry):
        cs = pl.ds(pl.multiple_of(c * LANES, LANES), LANES)
        for d in range(1, SUBLANES):
            sh_ref[d - 1] = ext_ref[pl.ds(d, TS + SH_EXTRA), cs]
        bias = b_ref[0, :, cs]
        accs = [jnp.broadcast_to(bias, (rc, LANES)) for _ in range(TS // rc)]
        for k in range(CW):
            q, d = divmod(k + off, SUBLANES)
            wk = jnp.broadcast_to(w_ref[0, k:k + 1, cs], (rc, LANES))
            for r in range(TS // rc):
                rows = pl.ds(r * rc + q * SUBLANES, rc)
                tap = ext_ref[rows, cs] if d == 0 else sh_ref[d - 1, rows, :]
                accs[r] = accs[r] + wk * tap
        for r in range(TS // rc):
            y_ref[pl.ds(r * rc, rc), cs] = accs[r]
        return carry

    lax.fori_loop(0, D // LANES, chan, 0)

    y = y_ref[...]
    yc = y - jnp.mean(y, axis=-1, keepdims=True)
    var = jnp.mean(yc * yc, axis=-1, keepdims=True)
    z = yc * lax.rsqrt(var + EPS) * lg_ref[0] + lb_ref[0]
    o_ref[...] = (z * _sigmoid(z)).astype(BF16)


def _conv_module(u, w_dw, b_dw, ln_g, ln_b, idx, t_rows):
    assert L == TS and TS % HALO == 0 and HALO >= CPAD
    hb = TS // HALO
    n_halo_blocks = u.shape[0] // HALO
    return pl.pallas_call(
        _conv_body,
        grid=(t_rows // TS,),
        in_specs=[
            pl.BlockSpec((HALO, D), lambda i: (jnp.maximum(i * hb - 1, 0), 0)),
            pl.BlockSpec((TS, D), lambda i: (i, 0)),
            pl.BlockSpec((HALO, D), lambda i: (jnp.minimum((i + 1) * hb, n_halo_blocks - 1), 0)),
            pl.BlockSpec((1, CW, D), lambda i: (idx, 0, 0)),
            _layer_vec_spec(idx),
            _layer_vec_spec(idx),
            _layer_vec_spec(idx),
        ],
        out_specs=pl.BlockSpec((TS, D), lambda i: (i, 0)),
        out_shape=jax.ShapeDtypeStruct((t_rows, D), BF16),
        scratch_shapes=[pltpu.VMEM((TS + 2 * HALO, D), F32), pltpu.VMEM((TS, D), F32),
                        pltpu.VMEM((SUBLANES - 1, TS + SH_EXTRA, LANES), F32)],
        compiler_params=_params("arbitrary"),
        name="dwconv_ln_silu",
    )(u, u, u, w_dw, b_dw, ln_g, ln_b)


def _router_body(x_ref, g_ref, sh_ref, sc_ref, wr_ref, hm_ref, info_ref):
    hm = _modulate(x_ref[...], g_ref[0], sh_ref[0], sc_ref[0])
    for s in range(SLABS):
        hm_ref[pl.ds(s, ROUTER_TM, stride=SLABS), :] = hm[:, s * LANES:(s + 1) * LANES]
    logits = jnp.dot(hm, wr_ref[...], precision=lax.Precision.HIGHEST, preferred_element_type=F32)
    lane = lax.broadcasted_iota(jnp.int32, logits.shape, 1).astype(F32)
    ninf = -jnp.inf
    lg = jnp.where(lane < E, logits, ninf)
    v1 = jnp.max(lg, axis=-1, keepdims=True)
    i1 = jnp.min(jnp.where(lg == v1, lane, float(LANES)), axis=-1, keepdims=True)
    lg2 = jnp.where(lane == i1, ninf, lg)
    v2 = jnp.max(lg2, axis=-1, keepdims=True)
    i2 = jnp.min(jnp.where(lg2 == v2, lane, float(LANES)), axis=-1, keepdims=True)
    e2 = jnp.exp(v2 - v1)
    w1 = 1.0 / (1.0 + e2)
    w2 = e2 * w1
    info_ref[...] = jnp.where(lane == 0, i1, jnp.where(lane == 1, i2,
                              jnp.where(lane == 2, w1, jnp.where(lane == 3, w2, 0.0))))


def _router(h, mods, g, layer, w_router_pad, t_rows):
    tm = ROUTER_TM
    return pl.pallas_call(
        _router_body,
        grid=(t_rows // tm,),
        in_specs=[
            pl.BlockSpec((tm, D), lambda i: (i, 0)),
            _layer_vec_spec(layer),
            _mod_spec(layer, 3, tm),
            _mod_spec(layer, 4, tm),
            pl.BlockSpec((D, LANES), lambda i: (0, 0)),
        ],
        out_specs=[pl.BlockSpec((tm * SLABS, LANES), lambda i: (i, 0)),
                   pl.BlockSpec((tm, LANES), lambda i: (i, 0))],
        out_shape=[jax.ShapeDtypeStruct((t_rows * SLABS, LANES), F32),
                   jax.ShapeDtypeStruct((t_rows, LANES), F32)],
        compiler_params=_params("arbitrary"),
        name="router",
    )(h, g, mods, mods, w_router_pad)


def _row_gather_step(i, n_steps, idx_ref, idx_next_ref, src_ref, buf, sem, rows, consume):
    def row_copy(src_row, r, slot):
        return pltpu.make_async_copy(
            src_ref.at[pl.ds(pl.multiple_of(src_row * SLABS, SLABS), SLABS)],
            buf.at[slot, pl.ds(pl.multiple_of(r * ROW_PITCH, SUBLANES), SLABS)],
            sem.at[slot])

    def issue(ref, slot):
        def body(r2, carry):
            for k in range(2):
                r = 2 * r2 + k
                row_copy(ref[0, 0, r], r, slot).start(priority=k)
            return carry
        lax.fori_loop(0, rows // 2, body, 0, unroll=4)

    def step(slot):
        if slot == 0:
            @pl.when(i == 0)
            def _():
                issue(idx_ref, 0)

        @pl.when(i + 1 < n_steps)
        def _():
            issue(idx_next_ref, 1 - slot)

        pltpu.make_async_copy(src_ref.at[pl.ds(0, rows * SLABS)], buf.at[slot, pl.ds(0, rows * SLABS)],
                              sem.at[slot]).wait()
        consume(slot)

    @pl.when(i % 2 == 0)
    def _():
        step(0)

    @pl.when(i % 2 == 1)
    def _():
        step(1)


def _gather_body(idx_ref, idx_next_ref, src_ref, o_ref, buf, sem):
    def consume(slot):
        for s in range(SLABS):
            o_ref[:, s * LANES:(s + 1) * LANES] = buf[slot, pl.ds(s, GATHER_ROWS, stride=ROW_PITCH), :].astype(BF16)

    _row_gather_step(pl.program_id(0), pl.num_programs(0), idx_ref, idx_next_ref, src_ref, buf, sem,
                     GATHER_ROWS, consume)


def _gather_rows(src_slabs, idx):
    n = idx.shape[0]
    assert n % GATHER_ROWS == 0
    nsteps = n // GATHER_ROWS
    idx3 = idx.reshape(nsteps, 1, GATHER_ROWS)
    return pl.pallas_call(
        _gather_body,
        grid=(nsteps,),
        in_specs=[pl.BlockSpec((1, 1, GATHER_ROWS), lambda i: (i, 0, 0), memory_space=pltpu.SMEM),
                  pl.BlockSpec((1, 1, GATHER_ROWS), lambda i: (jnp.minimum(i + 1, nsteps - 1), 0, 0),
                               memory_space=pltpu.SMEM),
                  pl.BlockSpec(memory_space=pl.ANY)],
        out_specs=pl.BlockSpec((GATHER_ROWS, D), lambda i: (i, 0)),
        out_shape=jax.ShapeDtypeStruct((n, D), BF16),
        scratch_shapes=[pltpu.VMEM((2, GATHER_ROWS * ROW_PITCH, LANES), F32), pltpu.SemaphoreType.DMA((2,))],
        compiler_params=_params("arbitrary"),
        name="moe_gather",
    )(idx3, idx3, src_slabs)


def _moe_up_body(te_ref, nu_ref, x_ref, wg_ref, wu_ref, o_ref, wg_bf, wu_bf):
    i = pl.program_id(1)

    @pl.when(i < nu_ref[0])
    def _():
        @pl.when(jnp.logical_or(i == 0, te_ref[i] != te_ref[jnp.maximum(i - 1, 0)]))
        def _():
            wg_bf[...] = wg_ref[0, 0].astype(BF16)
            wu_bf[...] = wu_ref[0, 0].astype(BF16)

        x = x_ref[...]
        for c in range(BN_UP // MXU_COLS):
            cs = slice(c * MXU_COLS, (c + 1) * MXU_COLS)
            a = jnp.dot(x, wg_bf[:, cs], preferred_element_type=F32)
            b = jnp.dot(x, wu_bf[:, cs], preferred_element_type=F32)
            o_ref[:, cs] = ((a * _sigmoid(a)) * b).astype(BF16)


def _moe_down_body(te_ref, nu_ref, a_ref, wd_ref, o_ref):
    @pl.when(pl.program_id(0) < nu_ref[0])
    def _():
        a = a_ref[...]
        for c in range(D // MXU_COLS):
            y = jnp.dot(a, wd_ref[0, :, c * MXU_COLS:(c + 1) * MXU_COLS], preferred_element_type=F32)
            for k in range(MXU_COLS // LANES):
                s = c * (MXU_COLS // LANES) + k
                o_ref[pl.ds(s, TME_DOWN, stride=SLABS), :] = y[:, k * LANES:(k + 1) * LANES]


def _moe_experts(xs, w_gu_all, moe_idx, w_down, tile_expert, n_used):
    n_tiles = xs.shape[0] // TME_UP
    nj = DFF // BN_UP

    def tile(i, nu):
        return jnp.minimum(i, nu[0] - 1)

    act = pl.pallas_call(
        _moe_up_body,
        grid_spec=pltpu.PrefetchScalarGridSpec(
            num_scalar_prefetch=2,
            grid=(nj, n_tiles),
            in_specs=[
                pl.BlockSpec((TME_UP, D), lambda j, i, te, nu: (tile(i, nu), 0)),
                pl.BlockSpec((1, 1, D, BN_UP), lambda j, i, te, nu: (moe_idx, te[tile(i, nu)], 0, j)),
                pl.BlockSpec((1, 1, D, BN_UP), lambda j, i, te, nu: (moe_idx, te[tile(i, nu)], 0, nj + j)),
            ],
            out_specs=pl.BlockSpec((TME_UP, BN_UP), lambda j, i, te, nu: (tile(i, nu), j)),
            scratch_shapes=[pltpu.VMEM((D, BN_UP), BF16), pltpu.VMEM((D, BN_UP), BF16)],
        ),
        out_shape=jax.ShapeDtypeStruct((n_tiles * TME_UP, DFF), BF16),
        compiler_params=_params("arbitrary", "arbitrary"),
        name="moe_up",
    )(tile_expert, n_used, xs, w_gu_all, w_gu_all)

    split = TME_UP // TME_DOWN
    return pl.pallas_call(
        _moe_down_body,
        grid_spec=pltpu.PrefetchScalarGridSpec(
            num_scalar_prefetch=2,
            grid=(n_tiles * split,),
            in_specs=[
                pl.BlockSpec((TME_DOWN, DFF), lambda i, te, nu: (tile(i, nu), 0)),
                pl.BlockSpec((1, DFF, D), lambda i, te, nu: (te[tile(i, nu)], 0, 0),
                             pipeline_mode=pl.Buffered(1)),
            ],
            out_specs=pl.BlockSpec((TME_DOWN * SLABS, LANES), lambda i, te, nu: (tile(i, nu), 0)),
        ),
        out_shape=jax.ShapeDtypeStruct((n_tiles * TME_UP * SLABS, LANES), F32),
        compiler_params=_params("arbitrary"),
        name="moe_down",
    )(jnp.repeat(tile_expert, split), n_used * split, act, w_down)


def _routing_tables(info, t_rows, n_tiles):
    idx = info[:, :2].astype(jnp.int32)
    flat_e = idx.T.reshape(-1)
    onehot = (flat_e[:, None] == jnp.arange(E, dtype=jnp.int32)[None, :]).astype(jnp.int32)
    csum = jnp.cumsum(onehot, axis=0)
    rank = jnp.sum((csum - onehot) * onehot, axis=1)
    counts = csum[-1]
    padded = ((counts + TME - 1) // TME) * TME
    ends = jnp.cumsum(padded)
    starts = ends - padded
    pos = starts[flat_e] + rank
    n_rows = n_tiles * TME
    tile_start = jnp.arange(n_tiles, dtype=jnp.int32) * TME
    tile_expert = jnp.minimum(jnp.sum((tile_start[:, None] >= ends[None, :]).astype(jnp.int32), axis=1), E - 1)
    n_used = (ends[-1] // TME).astype(jnp.int32).reshape(1)
    token_of_pos = jnp.zeros((n_rows,), jnp.int32).at[pos].set(
        jnp.arange(2 * t_rows, dtype=jnp.int32) % t_rows)
    return pos.astype(jnp.int32), token_of_pos, tile_expert.astype(jnp.int32), n_used


def _combine_body(idx_ref, idx_next_ref, ys_ref, h_ref, info_ref, gate_ref, *rest, final):
    if final:
        fg_ref, o_ref, buf, sem = rest
    else:
        o_ref, buf, sem = rest

    def consume(slot):
        w0 = jnp.broadcast_to(info_ref[:, 2:3], (COMBINE_TM, LANES))
        w1 = jnp.broadcast_to(info_ref[:, 3:4], (COMBINE_TM, LANES))
        for s in range(SLABS):
            cols = slice(s * LANES, (s + 1) * LANES)
            y = (w0 * buf[slot, pl.ds(s, COMBINE_TM, stride=ROW_PITCH), :]
                 + w1 * buf[slot, pl.ds(COMBINE_TM * ROW_PITCH + s, COMBINE_TM, stride=ROW_PITCH), :])
            o_ref[:, cols] = h_ref[:, cols] + gate_ref[0, :, cols] * y
        if final:
            h = o_ref[...]
            o_ref[...] = (h * lax.rsqrt(jnp.mean(h * h, axis=-1, keepdims=True) + EPS)) * fg_ref[...]

    _row_gather_step(pl.program_id(0), pl.num_programs(0), idx_ref, idx_next_ref, ys_ref, buf, sem,
                     2 * COMBINE_TM, consume)


def _combine(h, ys_slabs, pos, info, mods, layer, t_rows, final_g):
    tm = COMBINE_TM
    nt = t_rows // tm
    final = final_g is not None
    idx3 = pos.reshape(2, nt, tm).transpose(1, 0, 2).reshape(nt, 1, 2 * tm)
    in_specs = [pl.BlockSpec((1, 1, 2 * tm), lambda i: (i, 0, 0), memory_space=pltpu.SMEM),
                pl.BlockSpec((1, 1, 2 * tm), lambda i: (jnp.minimum(i + 1, nt - 1), 0, 0),
                             memory_space=pltpu.SMEM),
                pl.BlockSpec(memory_space=pl.ANY),
                pl.BlockSpec((tm, D), lambda i: (i, 0)),
                pl.BlockSpec((tm, LANES), lambda i: (i, 0)),
                _mod_spec(layer, 5, tm)]
    args = [idx3, idx3, ys_slabs, h, info, mods]
    if final:
        in_specs.append(pl.BlockSpec((1, D), lambda i: (0, 0)))
        args.append(final_g.reshape(1, D))
    return pl.pallas_call(
        functools.partial(_combine_body, final=final),
        grid=(nt,),
        in_specs=in_specs,
        out_specs=pl.BlockSpec((tm, D), lambda i: (i, 0)),
        out_shape=jax.ShapeDtypeStruct((t_rows, D), F32),
        scratch_shapes=[pltpu.VMEM((2, 2 * tm * ROW_PITCH, LANES), F32), pltpu.SemaphoreType.DMA((2,))],
        compiler_params=_params("arbitrary"),
        name="moe_combine",
    )(*args)


def _rope_tables():
    rows = S // GRID_W
    row = jnp.repeat(jnp.arange(rows, dtype=F32), GRID_W)
    col = jnp.tile(jnp.arange(GRID_W, dtype=F32), rows)
    n_freq = HD // 4
    inv_freq = ROPE_BASE ** (-jnp.arange(n_freq, dtype=F32) / n_freq)
    ang = jnp.concatenate([row[:, None] * inv_freq, col[:, None] * inv_freq], axis=-1)
    ang = jnp.concatenate([ang, ang], axis=-1)
    sign = jnp.where(jnp.arange(HD) < HD // 2, -1.0, 1.0).astype(F32)
    cos = jnp.concatenate([jnp.cos(ang), jnp.ones((TM, HD), F32)], axis=0)
    sin = jnp.concatenate([jnp.sin(ang) * sign, jnp.zeros((TM, HD), F32)], axis=0)
    scale = HD ** -0.5 * math.log2(math.e)
    return jnp.stack([cos * scale, cos]), jnp.stack([sin * scale, sin])


def kernel(x, c, ctx, c_ctx, ada_w, ada_b, norm_mix_g, norm_ffn_g, attn_w_qkv, attn_lambda, attn_subln_g, attn_w_o, conv_w_in, conv_b_in, conv_w_dw, conv_b_dw, conv_ln_g, conv_ln_b, conv_w_out, conv_b_out, ffn_w_gu, ffn_w_down, moe_router, moe_w_gu, moe_w_down, final_g):
    x_lat, x_ctx = x.reshape(T_LAT, D), ctx.reshape(T_CTX, D)
    h = None
    cond =jnp.concatenate([c, c_ctx[None, :], jnp.zeros((COND_ROWS - B - 1, D), F32)], axis=0)
    mods = _ada_mods(cond, ada_w, ada_b)
    cos_t, sin_t = _rope_tables()
    mix_g = norm_mix_g.reshape(DEPTH, 1, D)
    ffn_g = norm_ffn_g.reshape(DEPTH, 1, D)

    for i in range(DEPTH):
        mix_idx = i // 2
        ffn_idx = i // 2
        ctx_live = i < LAST_CTX_READER
        t_rows = T_ALL if ctx_live else T_LAT
        if i % 2 == 0:
            lambda_init = 0.8 - 0.6 * math.exp(-0.3 * i)
            w_qkv = attn_w_qkv[mix_idx].astype(BF16)
            w_o = attn_w_o[mix_idx].astype(BF16)
            if h is None:
                qkv = _qkv_proj(x_lat, mods, mix_g, i, w_qkv, cos_t, sin_t)
                qkv = _qkv_proj(x_ctx, mods, mix_g, i, w_qkv, cos_t, sin_t, row_off=T_LAT, out=qkv)
                o = _attention(qkv, attn_lambda, attn_subln_g, mix_idx, lambda_init, ctx_live)
                h = _resid_proj(o, w_o, None, x_lat, mods, i, 2, t_rows, "attn_out", n_rows=T_LAT)
                h = _resid_proj(o, w_o, None, x_ctx, mods, i, 2, t_rows, "attn_out", n_rows=T_CTX,
                                row_off=T_LAT, out=h)
            else:
                qkv = _qkv_proj(h, mods, mix_g, i, w_qkv, cos_t, sin_t)
                o = _attention(qkv, attn_lambda, attn_subln_g, mix_idx, lambda_init, ctx_live)
                h = _resid_proj(o, w_o, None, h, mods, i, 2, t_rows, "attn_out")
        else:
            u = _glu_proj(h, mods, mix_g, i, conv_w_in[mix_idx].astype(BF16),
                          (mix_idx, conv_b_in.reshape(-1, 1, 2 * D)), t_rows, "glu", F32, TM, BN_CONV_IN, "conv_in")
            z = _conv_module(u, conv_w_dw, conv_b_dw.reshape(-1, 1, D), conv_ln_g.reshape(-1, 1, D),
                             conv_ln_b.reshape(-1, 1, D), mix_idx, t_rows)
            h = _resid_proj(z, conv_w_out[mix_idx].astype(BF16), (mix_idx, conv_b_out.reshape(-1, 1, D)),
                            h, mods, i, 2, t_rows, "conv_out")
        if i % 2 == 0:
            act = _glu_proj(h, mods, ffn_g, i, ffn_w_gu[ffn_idx].astype(BF16), None, t_rows,
                            "swiglu", BF16, TM_FFN_UP, BN_FFN_UP, "ffn_up")
            h = _resid_proj(act, ffn_w_down[ffn_idx].astype(BF16), None, h, mods, i, 5, t_rows, "ffn_down")
        else:
            n_tiles = N_TILES_E if ctx_live else N_TILES_E_LAT
            w_r = jnp.zeros((D, LANES), F32).at[:, :E].set(moe_router[ffn_idx])
            hm_slabs, info = _router(h, mods, ffn_g, i, w_r, t_rows)
            pos, token_of_pos, tile_expert, n_used = _routing_tables(info, t_rows, n_tiles)
            xs = _gather_rows(hm_slabs, token_of_pos)
            ys_slabs = _moe_experts(xs, moe_w_gu, ffn_idx, moe_w_down[ffn_idx].astype(BF16),
                                    tile_expert, n_used)
            h = _combine(h, ys_slabs, pos, info, mods, i, t_rows, final_g if i == DEPTH - 1 else None)
    return h.reshape(B, S, D)
```

```python
import functools
import math

import jax
import jax.numpy as jnp
from jax import lax
from jax.experimental import pallas as pl
from jax.experimental.pallas import tpu as pltpu

F32 = jnp.float32
BF16 = jnp.bfloat16

D = 2048
B = 16
S = 2048
L = 256
DEPTH = 4
GRID_W = 64
H = 8
HD = 128
VD = 2 * HD
ROPE_BASE = 10000.0
CW = 31
CPAD = (CW - 1) // 2
DFF = 5632
E = 8
NMOD = 6
EPS = 1e-6

T_LAT = B * S
T_CTX = B * L
T_ALL = T_LAT + T_CTX
COND_ROWS = 24
CTX_COND_ROW = B
LAST_CTX_READER = 2

V7X_VMEM_LIMIT_BYTES = 56 * 1024 * 1024
LANES = 128
SUBLANES = 8
HALO = 16
SH_EXTRA = SUBLANES * ((CW - 1 + HALO - CPAD) // SUBLANES)

TM = 1024
BN_FFN_UP = 512
BN_CONV_IN = 1024
BN_QKV = 2048
TM_RESID = 512
TQ = 1024
TQ_SUB = 128
TS = 256
MXU_COLS = 256
TME = 1024
TME_UP = TME
TME_DOWN = 512
BN_UP = 512
SLABS = D // LANES
ROW_PITCH = SLABS + SUBLANES
ROUTER_TM = 512
GATHER_ROWS = 512
COMBINE_TM = 256
N_TILES_E = (2 * T_ALL + E * (TME - 1) + TME - 1) // TME
N_TILES_E_LAT = (2 * T_LAT + E * (TME - 1) + TME - 1) // TME


def _params(*sem):
    return pltpu.CompilerParams(dimension_semantics=sem, vmem_limit_bytes=V7X_VMEM_LIMIT_BYTES)


def _sigmoid(x):
    return 1.0 / (1.0 + jnp.exp(-x))


def _modulate(x, g, shift, scale):
    xn = x * lax.rsqrt(jnp.mean(x * x, axis=-1, keepdims=True) + EPS)
    return (xn * g) * (1.0 + scale) + shift


def _modulate_into(x_ref, g_ref, sh_ref, sc_ref, out_ref):
    g, shift, scale = g_ref[0], sh_ref[0], sc_ref[0]
    rb = 2 * SUBLANES

    def body(r, carry):
        rows = pl.ds(pl.multiple_of(r * rb, rb), rb)
        out_ref[rows, :] = _modulate(x_ref[rows, :], g, shift, scale).astype(out_ref.dtype)
        return carry

    lax.fori_loop(0, x_ref.shape[0] // rb, body, 0, unroll=8)


def _cond_row(i, tm):
    return jnp.where(i * tm < T_LAT, (i * tm) // S, CTX_COND_ROW)


def _mod_spec(layer, which, tm, tile_off=0):
    def imap(i, *rest):
        return ((layer * COND_ROWS + _cond_row(i + tile_off, tm)) * NMOD + which, 0, 0)
    return pl.BlockSpec((1, 1, D), imap)


def _layer_vec_spec(idx):
    return pl.BlockSpec((1, 1, D), lambda *grid: (idx, 0, 0))


def _ada_body(c_ref, w_ref, b_ref, o_ref):
    x = c_ref[...]
    s = (x * _sigmoid(x)).astype(BF16)
    o_ref[0] = jnp.dot(s, w_ref[0].astype(BF16), preferred_element_type=F32) + b_ref[0]


def _ada_mods(cond, ada_w, ada_b):
    bn = 1024
    n = NMOD * D
    out = pl.pallas_call(
        _ada_body,
        grid=(DEPTH, n // bn),
        in_specs=[
            pl.BlockSpec((COND_ROWS, D), lambda l, j: (0, 0)),
            pl.BlockSpec((1, D, bn), lambda l, j: (l, 0, j)),
            pl.BlockSpec((1, 1, bn), lambda l, j: (l, 0, j)),
        ],
        out_specs=pl.BlockSpec((1, COND_ROWS, bn), lambda l, j: (l, 0, j)),
        out_shape=jax.ShapeDtypeStruct((DEPTH, COND_ROWS, n), F32),
        compiler_params=_params("arbitrary", "arbitrary"),
        name="ada_mods",
    )(cond, ada_w, ada_b.reshape(DEPTH, 1, n))
    return out.reshape(DEPTH * COND_ROWS * NMOD, 1, D)


def _qkv_body(x_ref, g_ref, sh_ref, sc_ref, w_ref, cos_ref, sin_ref, *rest, n_rope):
    o_ref, hm_ref = rest[-2:]
    j = pl.program_id(1)

    @pl.when(j == 0)
    def _():
        _modulate_into(x_ref, g_ref, sh_ref, sc_ref, hm_ref)

    hm = hm_ref[...]

    def chunk(c):
        return jnp.dot(hm, w_ref[:, c * MXU_COLS:(c + 1) * MXU_COLS], preferred_element_type=F32)

    @pl.when(j < n_rope)
    def _():
        cos = cos_ref[0]
        sin = sin_ref[0]
        for c in range(BN_QKV // MXU_COLS):
            acc = chunk(c)
            for k in range(MXU_COLS // HD):
                xc = acc[:, k * HD:(k + 1) * HD]
                col = c * MXU_COLS + k * HD
                o_ref[:, col:col + HD] = (xc * cos + pltpu.roll(xc, HD // 2, 1) * sin).astype(BF16)

    @pl.when(j >= n_rope)
    def _():
        for c in range(BN_QKV // MXU_COLS):
            o_ref[:, c * MXU_COLS:(c + 1) * MXU_COLS] = chunk(c).astype(BF16)


def _qkv_proj(h, mods, g, layer, w_qkv, cos_t, sin_t, row_off=0, out=None):
    off = row_off // TM
    nq = D // BN_QKV
    tiles_per_seq = S // TM

    def rope_map(i, j):
        gi = i + off
        return (jnp.where(j < nq, 0, 1),
                jnp.where(gi * TM < T_LAT, gi % tiles_per_seq, tiles_per_seq), 0)

    in_specs = [
        pl.BlockSpec((TM, D), lambda i, j: (i, 0)),
        _layer_vec_spec(layer),
        _mod_spec(layer, 0, TM, off),
        _mod_spec(layer, 1, TM, off),
        pl.BlockSpec((D, BN_QKV), lambda i, j: (0, j)),
        pl.BlockSpec((1, TM, HD), rope_map),
        pl.BlockSpec((1, TM, HD), rope_map),
    ]
    args = [h, g, mods, mods, w_qkv, cos_t, sin_t]
    aliases = {}
    if out is not None:
        in_specs.append(pl.BlockSpec(memory_space=pl.ANY))
        args.append(out)
        aliases = {len(args) - 1: 0}
    return pl.pallas_call(
        functools.partial(_qkv_body, n_rope=2 * nq),
        grid=(h.shape[0] // TM, 3 * D // BN_QKV),
        in_specs=in_specs,
        out_specs=pl.BlockSpec((TM, BN_QKV), lambda i, j: (i + off, j)),
        out_shape=jax.ShapeDtypeStruct((T_ALL, 3 * D), BF16),
        scratch_shapes=[pltpu.VMEM((TM, D), BF16)],
        input_output_aliases=aliases,
        compiler_params=_params("arbitrary", "arbitrary"),
        name="qkv_proj",
    )(*args)


def _attn_body(lam_ref, g_ref, q_ref, kc_ref, vc_ref, *rest, lambda_init, with_lat):
    if with_lat:
        k_ref, v_ref, o_ref = rest
    else:
        o_ref = rest[-1]
    lp = lam_ref[0]
    lam = (jnp.exp(jnp.sum(lp[0:1] * lp[1:2], axis=-1, keepdims=True))
           - jnp.exp(jnp.sum(lp[2:3] * lp[3:4], axis=-1, keepdims=True)) + lambda_init)
    nt_dims = (((1,), (1,)), ((), ()))
    n_rows = q_ref.shape[0]
    sub = min(n_rows, TQ_SUB)
    def scores(t):
        out = []
        for c in range(2):
            sl = slice(c * HD, (c + 1) * HD)
            qc = q_ref[t * sub:(t + 1) * sub, sl]
            s_c = lax.dot_general(qc, kc_ref[:, sl], nt_dims, preferred_element_type=F32)
            s_l = lax.dot_general(qc, k_ref[:, sl], nt_dims, preferred_element_type=F32) if with_lat else None
            out.append((s_c, s_l))
        return out

    def weights(sc):
        parts = []
        for s_c, s_l in sc:
            m = jnp.max(s_c, axis=-1, keepdims=True)
            if with_lat:
                m = jnp.maximum(m, jnp.max(s_l, axis=-1, keepdims=True))
                p_l = jnp.exp2(s_l - m)
            p_c = jnp.exp2(s_c - m)
            den = jnp.sum(p_c, axis=-1, keepdims=True)
            if with_lat:
                den = den + jnp.sum(p_l, axis=-1, keepdims=True)
            parts.append((p_c, p_l if with_lat else None, den))
        (p1c, p1l, den1), (p2c, p2l, den2) = parts
        r = lam * den1 / den2
        a_c = (p1c - r * p2c).astype(BF16)
        a_l = (p1l - r * p2l).astype(BF16) if with_lat else None
        return a_c, a_l, den1

    def output(t, w):
        a_c, a_l, den1 = w
        o = jnp.dot(a_c, vc_ref[...], preferred_element_type=F32)
        if with_lat:
            o = o + jnp.dot(a_l, v_ref[...], preferred_element_type=F32)
        o = o / den1
        o = o * lax.rsqrt(jnp.mean(o * o, axis=-1, keepdims=True) + EPS) * g_ref[0] * (1.0 - lambda_init)
        o_ref[t * sub:(t + 1) * sub, :] = o.astype(BF16)

    n_sub = n_rows // sub
    nxt = scores(0)
    for t in range(n_sub):
        cur = nxt
        if t + 1 < n_sub:
            nxt = scores(t + 1)
        output(t, weights(cur))


def _attention(qkv, lam_p, subln_g, mix_idx, lambda_init, ctx_out):
    t_out = T_ALL if ctx_out else T_LAT
    nqt = S // TQ
    kcol, vcol = D // VD, 2 * D // VD
    ctx_blk = T_LAT // L
    common = [
        pl.BlockSpec((1, 4, HD), lambda b, h, t: (mix_idx, 0, 0)),
        pl.BlockSpec((1, 1, VD), lambda b, h, t: (mix_idx, 0, 0)),
    ]
    ctx_kv = [
        pl.BlockSpec((L, VD), lambda b, h, t: (ctx_blk + b, kcol + h)),
        pl.BlockSpec((L, VD), lambda b, h, t: (ctx_blk + b, vcol + h)),
    ]
    g3 = subln_g.reshape(-1, 1, VD)
    o = pl.pallas_call(
        functools.partial(_attn_body, lambda_init=lambda_init, with_lat=True),
        grid=(B, H, nqt),
        in_specs=common + [pl.BlockSpec((TQ, VD), lambda b, h, t: (b * nqt + t, h))] + ctx_kv + [
            pl.BlockSpec((S, VD), lambda b, h, t: (b, kcol + h)),
            pl.BlockSpec((S, VD), lambda b, h, t: (b, vcol + h)),
        ],
        out_specs=pl.BlockSpec((TQ, VD), lambda b, h, t: (b * nqt + t, h)),
        out_shape=jax.ShapeDtypeStruct((t_out, D), BF16),
        compiler_params=_params("arbitrary", "arbitrary", "arbitrary"),
        name="attn_lat",
    )(lam_p, g3, qkv, qkv, qkv, qkv, qkv)
    if not ctx_out:
        return o
    return pl.pallas_call(
        functools.partial(_attn_body, lambda_init=lambda_init, with_lat=False),
        grid=(B, H, 1),
        in_specs=common + [pl.BlockSpec((L, VD), lambda b, h, t: (ctx_blk + b, h))] + ctx_kv + [
            pl.BlockSpec(memory_space=pl.ANY),
        ],
        out_specs=pl.BlockSpec((L, VD), lambda b, h, t: (ctx_blk + b, h)),
        out_shape=jax.ShapeDtypeStruct((t_out, D), BF16),
        input_output_aliases={5: 0},
        compiler_params=_params("arbitrary", "arbitrary", "arbitrary"),
        name="attn_ctx",
    )(lam_p, g3, qkv, qkv, qkv, o)


def _resid_body(x_ref, w_ref, *rest, has_bias):
    o_ref = rest[-1]
    if has_bias:
        b_ref, r_ref, gate_ref = rest[:3]
    else:
        r_ref, gate_ref = rest[:2]
    x = x_ref[...]
    for c in range(D // MXU_COLS):
        cs = slice(c * MXU_COLS, (c + 1) * MXU_COLS)
        y = jnp.dot(x, w_ref[:, cs], preferred_element_type=F32)
        if has_bias:
            y = y + b_ref[0, :, cs]
        o_ref[:, cs] = r_ref[:, cs] + gate_ref[0, :, cs] * y


def _resid_proj(x, w, bias, resid, mods, layer, which, t_out, name, n_rows=None, row_off=0, out=None):
    k = x.shape[1]
    tm = TM_RESID if k > D else 2 * TM_RESID
    off = row_off // tm
    n_rows = t_out if n_rows is None else n_rows
    has_bias = bias is not None
    in_specs = [pl.BlockSpec((tm, k), lambda i: (i + off, 0)),
                pl.BlockSpec((k, D), lambda i: (0, 0), pipeline_mode=pl.Buffered(1))]
    args = [x, w]
    if has_bias:
        idx, arr = bias
        in_specs.append(_layer_vec_spec(idx))
        args.append(arr)
    in_specs += [pl.BlockSpec((tm, D), lambda i: (i, 0)), _mod_spec(layer, which, tm, off)]
    args += [resid, mods]
    aliases = {}
    if out is not None:
        in_specs.append(pl.BlockSpec(memory_space=pl.ANY))
        args.append(out)
        aliases = {len(args) - 1: 0}
    return pl.pallas_call(
        functools.partial(_resid_body, has_bias=has_bias),
        grid=(n_rows // tm,),
        in_specs=in_specs,
        out_specs=pl.BlockSpec((tm, D), lambda i: (i + off, 0)),
        out_shape=jax.ShapeDtypeStruct((t_out, D), F32),
        input_output_aliases=aliases,
        compiler_params=_params("arbitrary"),
        name=name,
    )(*args)


def _glu_body(x_ref, g_ref, sh_ref, sc_ref, wa_ref, wb_ref, *rest, kind):
    if kind == "glu":
        ba_ref, bb_ref, o_ref, hm_ref = rest
    else:
        o_ref, hm_ref = rest

    @pl.when(pl.program_id(1) == 0)
    def _():
        _modulate_into(x_ref, g_ref, sh_ref, sc_ref, hm_ref)

    hm = hm_ref[...]
    for c in range(wa_ref.shape[1] // MXU_COLS):
        cs = slice(c * MXU_COLS, (c + 1) * MXU_COLS)
        a = jnp.dot(hm, wa_ref[:, cs], preferred_element_type=F32)
        b = jnp.dot(hm, wb_ref[:, cs], preferred_element_type=F32)
        if kind == "glu":
            o = (a + ba_ref[0, :, cs]) * _sigmoid(b + bb_ref[0, :, cs])
        else:
            o = (a * _sigmoid(a)) * b
        o_ref[:, cs] = o.astype(o_ref.dtype)


def _glu_proj(h, mods, g, layer, w, bias, t_rows, kind, out_dtype, bn, name):
    nh = w.shape[1] // 2
    nj = nh // bn
    in_specs = [
        pl.BlockSpec((TM, D), lambda i, j: (i, 0)),
        _layer_vec_spec(layer),
        _mod_spec(layer, 3 if kind == "swiglu" else 0, TM),
        _mod_spec(layer, 4 if kind == "swiglu" else 1, TM),
        pl.BlockSpec((D, bn), lambda i, j: (0, j)),
        pl.BlockSpec((D, bn), lambda i, j: (0, nj + j)),
    ]
    args = [h, g, mods, mods, w, w]
    if kind == "glu":
        idx, arr = bias
        in_specs += [pl.BlockSpec((1, 1, bn), lambda i, j: (idx, 0, j)),
                     pl.BlockSpec((1, 1, bn), lambda i, j: (idx, 0, nj + j))]
        args += [arr, arr]
    return pl.pallas_call(
        functools.partial(_glu_body, kind=kind),
        grid=(t_rows // TM, nj),
        in_specs=in_specs,
        out_specs=pl.BlockSpec((TM, bn), lambda i, j: (i, j)),
        out_shape=jax.ShapeDtypeStruct((t_rows, nh), out_dtype),
        scratch_shapes=[pltpu.VMEM((TM, D), BF16)],
        compiler_params=_params("arbitrary", "arbitrary"),
        name=name,
    )(*args)


def _conv_body(prev_ref, cur_ref, next_ref, w_ref, b_ref, lg_ref, lb_ref, o_ref, ext_ref, y_ref, sh_ref):
    i = pl.program_id(0)
    tiles_per_seq = S // TS
    is_ctx = i * TS >= T_LAT
    first = jnp.logical_or(is_ctx, i % tiles_per_seq == 0)
    last = jnp.logical_or(is_ctx, i % tiles_per_seq == tiles_per_seq - 1)
    ext_ref[0:HALO, :] = jnp.where(first, 0.0, prev_ref[...])
    ext_ref[HALO:HALO + TS, :] = cur_ref[...]
    ext_ref[HALO + TS:2 * HALO + TS, :] = jnp.where(last, 0.0, next_ref[...])

    rc = 64
    off = HALO - CPAD

    def chan(c, carry):
        cs = pl.ds(pl.multiple_of(c * LANES, LANES), LANES)
        for d in range(1, SUBLANES):
            sh_ref[d - 1] = ext_ref[pl.ds(d, TS + SH_EXTRA), cs]
        bias = b_ref[0, :, cs]
        accs = [jnp.broadcast_to(bias, (rc, LANES)) for _ in range(TS // rc)]
        for k in range(CW):
            q, d = divmod(k + off, SUBLANES)
            wk = jnp.broadcast_to(w_ref[0, k:k + 1, cs], (rc, LANES))
            for r in range(TS // rc):
                rows = pl.ds(r * rc + q * SUBLANES, rc)
                tap = ext_ref[rows, cs] if d == 0 else sh_ref[d - 1, rows, :]
                accs[r] = accs[r] + wk * tap
        for r in range(TS // rc):
            y_ref[pl.ds(r * rc, rc), cs] = accs[r]
        return carry

    lax.fori_loop(0, D // LANES, chan, 0)

    y = y_ref[...]
    yc = y - jnp.mean(y, axis=-1, keepdims=True)
    var = jnp.mean(yc * yc, axis=-1, keepdims=True)
    z = yc * lax.rsqrt(var + EPS) * lg_ref[0] + lb_ref[0]
    o_ref[...] = (z * _sigmoid(z)).astype(BF16)


def _conv_module(u, w_dw, b_dw, ln_g, ln_b, idx, t_rows):
    assert L == TS and TS % HALO == 0 and HALO >= CPAD
    hb = TS // HALO
    n_halo_blocks = u.shape[0] // HALO
    return pl.pallas_call(
        _conv_body,
        grid=(t_rows // TS,),
        in_specs=[
            pl.BlockSpec((HALO, D), lambda i: (jnp.maximum(i * hb - 1, 0), 0)),
            pl.BlockSpec((TS, D), lambda i: (i, 0)),
            pl.BlockSpec((HALO, D), lambda i: (jnp.minimum((i + 1) * hb, n_halo_blocks - 1), 0)),
            pl.BlockSpec((1, CW, D), lambda i: (idx, 0, 0)),
            _layer_vec_spec(idx),
            _layer_vec_spec(idx),
            _layer_vec_spec(idx),
        ],
        out_specs=pl.BlockSpec((TS, D), lambda i: (i, 0)),
        out_shape=jax.ShapeDtypeStruct((t_rows, D), BF16),
        scratch_shapes=[pltpu.VMEM((TS + 2 * HALO, D), F32), pltpu.VMEM((TS, D), F32),
                        pltpu.VMEM((SUBLANES - 1, TS + SH_EXTRA, LANES), F32)],
        compiler_params=_params("arbitrary"),
        name="dwconv_ln_silu",
    )(u, u, u, w_dw, b_dw, ln_g, ln_b)


def _router_body(x_ref, g_ref, sh_ref, sc_ref, wr_ref, hm_ref, info_ref):
    hm = _modulate(x_ref[...], g_ref[0], sh_ref[0], sc_ref[0])
    for s in range(SLABS):
        hm_ref[pl.ds(s, ROUTER_TM, stride=SLABS), :] = hm[:, s * LANES:(s + 1) * LANES]
    logits = jnp.dot(hm, wr_ref[...], precision=lax.Precision.HIGHEST, preferred_element_type=F32)
    lane = lax.broadcasted_iota(jnp.int32, logits.shape, 1).astype(F32)
    ninf = -jnp.inf
    lg = jnp.where(lane < E, logits, ninf)
    v1 = jnp.max(lg, axis=-1, keepdims=True)
    i1 = jnp.min(jnp.where(lg == v1, lane, float(LANES)), axis=-1, keepdims=True)
    lg2 = jnp.where(lane == i1, ninf, lg)
    v2 = jnp.max(lg2, axis=-1, keepdims=True)
    i2 = jnp.min(jnp.where(lg2 == v2, lane, float(LANES)), axis=-1, keepdims=True)
    e2 = jnp.exp(v2 - v1)
    w1 = 1.0 / (1.0 + e2)
    w2 = e2 * w1
    info_ref[...] = jnp.where(lane == 0, i1, jnp.where(lane == 1, i2,
                              jnp.where(lane == 2, w1, jnp.where(lane == 3, w2, 0.0))))


def _router(h, mods, g, layer, w_router_pad, t_rows):
    tm = ROUTER_TM
    return pl.pallas_call(
        _router_body,
        grid=(t_rows // tm,),
        in_specs=[
            pl.BlockSpec((tm, D), lambda i: (i, 0)),
            _layer_vec_spec(layer),
            _mod_spec(layer, 3, tm),
            _mod_spec(layer, 4, tm),
            pl.BlockSpec((D, LANES), lambda i: (0, 0)),
        ],
        out_specs=[pl.BlockSpec((tm * SLABS, LANES), lambda i: (i, 0)),
                   pl.BlockSpec((tm, LANES), lambda i: (i, 0))],
        out_shape=[jax.ShapeDtypeStruct((t_rows * SLABS, LANES), F32),
                   jax.ShapeDtypeStruct((t_rows, LANES), F32)],
        compiler_params=_params("arbitrary"),
        name="router",
    )(h, g, mods, mods, w_router_pad)


def _row_gather_step(i, n_steps, idx_ref, idx_next_ref, src_ref, buf, sem, rows, consume):
    def row_copy(src_row, r, slot):
        return pltpu.make_async_copy(
            src_ref.at[pl.ds(pl.multiple_of(src_row * SLABS, SLABS), SLABS)],
            buf.at[slot, pl.ds(pl.multiple_of(r * ROW_PITCH, SUBLANES), SLABS)],
            sem.at[slot])

    def issue(ref, slot):
        def body(r2, carry):
            for k in range(2):
                r = 2 * r2 + k
                row_copy(ref[0, 0, r], r, slot).start(priority=k)
            return carry
        lax.fori_loop(0, rows // 2, body, 0, unroll=4)

    def step(slot):
        if slot == 0:
            @pl.when(i == 0)
            def _():
                issue(idx_ref, 0)

        @pl.when(i + 1 < n_steps)
        def _():
            issue(idx_next_ref, 1 - slot)

        pltpu.make_async_copy(src_ref.at[pl.ds(0, rows * SLABS)], buf.at[slot, pl.ds(0, rows * SLABS)],
                              sem.at[slot]).wait()
        consume(slot)

    @pl.when(i % 2 == 0)
    def _():
        step(0)

    @pl.when(i % 2 == 1)
    def _():
        step(1)


def _gather_body(idx_ref, idx_next_ref, src_ref, o_ref, buf, sem):
    def consume(slot):
        for s in range(SLABS):
            o_ref[:, s * LANES:(s + 1) * LANES] = buf[slot, pl.ds(s, GATHER_ROWS, stride=ROW_PITCH), :].astype(BF16)

    _row_gather_step(pl.program_id(0), pl.num_programs(0), idx_ref, idx_next_ref, src_ref, buf, sem,
                     GATHER_ROWS, consume)


def _gather_rows(src_slabs, idx):
    n = idx.shape[0]
    assert n % GATHER_ROWS == 0
    nsteps = n // GATHER_ROWS
    idx3 = idx.reshape(nsteps, 1, GATHER_ROWS)
    return pl.pallas_call(
        _gather_body,
        grid=(nsteps,),
        in_specs=[pl.BlockSpec((1, 1, GATHER_ROWS), lambda i: (i, 0, 0), memory_space=pltpu.SMEM),
                  pl.BlockSpec((1, 1, GATHER_ROWS), lambda i: (jnp.minimum(i + 1, nsteps - 1), 0, 0),
                               memory_space=pltpu.SMEM),
                  pl.BlockSpec(memory_space=pl.ANY)],
        out_specs=pl.BlockSpec((GATHER_ROWS, D), lambda i: (i, 0)),
        out_shape=jax.ShapeDtypeStruct((n, D), BF16),
        scratch_shapes=[pltpu.VMEM((2, GATHER_ROWS * ROW_PITCH, LANES), F32), pltpu.SemaphoreType.DMA((2,))],
        compiler_params=_params("arbitrary"),
        name="moe_gather",
    )(idx3, idx3, src_slabs)


def _moe_up_body(te_ref, nu_ref, x_ref, wg_ref, wu_ref, o_ref, wg_bf, wu_bf):
    i = pl.program_id(1)

    @pl.when(i < nu_ref[0])
    def _():
        @pl.when(jnp.logical_or(i == 0, te_ref[i] != te_ref[jnp.maximum(i - 1, 0)]))
        def _():
            wg_bf[...] = wg_ref[0, 0].astype(BF16)
            wu_bf[...] = wu_ref[0, 0].astype(BF16)

        x = x_ref[...]
        for c in range(BN_UP // MXU_COLS):
            cs = slice(c * MXU_COLS, (c + 1) * MXU_COLS)
            a = jnp.dot(x, wg_bf[:, cs], preferred_element_type=F32)
            b = jnp.dot(x, wu_bf[:, cs], preferred_element_type=F32)
            o_ref[:, cs] = ((a * _sigmoid(a)) * b).astype(BF16)


def _moe_down_body(te_ref, nu_ref, a_ref, wd_ref, o_ref):
    @pl.when(pl.program_id(0) < nu_ref[0])
    def _():
        a = a_ref[...]
        for c in range(D // MXU_COLS):
            y = jnp.dot(a, wd_ref[0, :, c * MXU_COLS:(c + 1) * MXU_COLS], preferred_element_type=F32)
            for k in range(MXU_COLS // LANES):
                s = c * (MXU_COLS // LANES) + k
                o_ref[pl.ds(s, TME_DOWN, stride=SLABS), :] = y[:, k * LANES:(k + 1) * LANES]


def _moe_experts(xs, w_gu_all, moe_idx, w_down, tile_expert, n_used):
    n_tiles = xs.shape[0] // TME_UP
    nj = DFF // BN_UP

    def tile(i, nu):
        return jnp.minimum(i, nu[0] - 1)

    act = pl.pallas_call(
        _moe_up_body,
        grid_spec=pltpu.PrefetchScalarGridSpec(
            num_scalar_prefetch=2,
            grid=(nj, n_tiles),
            in_specs=[
                pl.BlockSpec((TME_UP, D), lambda j, i, te, nu: (tile(i, nu), 0)),
                pl.BlockSpec((1, 1, D, BN_UP), lambda j, i, te, nu: (moe_idx, te[tile(i, nu)], 0, j)),
                pl.BlockSpec((1, 1, D, BN_UP), lambda j, i, te, nu: (moe_idx, te[tile(i, nu)], 0, nj + j)),
            ],
            out_specs=pl.BlockSpec((TME_UP, BN_UP), lambda j, i, te, nu: (tile(i, nu), j)),
            scratch_shapes=[pltpu.VMEM((D, BN_UP), BF16), pltpu.VMEM((D, BN_UP), BF16)],
        ),
        out_shape=jax.ShapeDtypeStruct((n_tiles * TME_UP, DFF), BF16),
        compiler_params=_params("arbitrary", "arbitrary"),
        name="moe_up",
    )(tile_expert, n_used, xs, w_gu_all, w_gu_all)

    split = TME_UP // TME_DOWN
    return pl.pallas_call(
        _moe_down_body,
        grid_spec=pltpu.PrefetchScalarGridSpec(
            num_scalar_prefetch=2,
            grid=(n_tiles * split,),
            in_specs=[
                pl.BlockSpec((TME_DOWN, DFF), lambda i, te, nu: (tile(i, nu), 0)),
                pl.BlockSpec((1, DFF, D), lambda i, te, nu: (te[tile(i, nu)], 0, 0),
                             pipeline_mode=pl.Buffered(1)),
            ],
            out_specs=pl.BlockSpec((TME_DOWN * SLABS, LANES), lambda i, te, nu: (tile(i, nu), 0)),
        ),
        out_shape=jax.ShapeDtypeStruct((n_tiles * TME_UP * SLABS, LANES), F32),
        compiler_params=_params("arbitrary"),
        name="moe_down",
    )(jnp.repeat(tile_expert, split), n_used * split, act, w_down)


def _routing_tables(info, t_rows, n_tiles):
    idx = info[:, :2].astype(jnp.int32)
    flat_e = idx.T.reshape(-1)
    onehot = (flat_e[:, None] == jnp.arange(E, dtype=jnp.int32)[None, :]).astype(jnp.int32)
    csum = jnp.cumsum(onehot, axis=0)
    rank = jnp.sum((csum - onehot) * onehot, axis=1)
    counts = csum[-1]
    padded = ((counts + TME - 1) // TME) * TME
    ends = jnp.cumsum(padded)
    starts = ends - padded
    pos = starts[flat_e] + rank
    n_rows = n_tiles * TME
    tile_start = jnp.arange(n_tiles, dtype=jnp.int32) * TME
    tile_expert = jnp.minimum(jnp.sum((tile_start[:, None] >= ends[None, :]).astype(jnp.int32), axis=1), E - 1)
    n_used = (ends[-1] // TME).astype(jnp.int32).reshape(1)
    token_of_pos = jnp.zeros((n_rows,), jnp.int32).at[pos].set(
        jnp.arange(2 * t_rows, dtype=jnp.int32) % t_rows)
    return pos.astype(jnp.int32), token_of_pos, tile_expert.astype(jnp.int32), n_used


def _combine_body(idx_ref, idx_next_ref, ys_ref, h_ref, info_ref, gate_ref, *rest, final):
    if final:
        fg_ref, o_ref, buf, sem = rest
    else:
        o_ref, buf, sem = rest

    def consume(slot):
        w0 = jnp.broadcast_to(info_ref[:, 2:3], (COMBINE_TM, LANES))
        w1 = jnp.broadcast_to(info_ref[:, 3:4], (COMBINE_TM, LANES))
        for s in range(SLABS):
            cols = slice(s * LANES, (s + 1) * LANES)
            y = (w0 * buf[slot, pl.ds(s, COMBINE_TM, stride=ROW_PITCH), :]
                 + w1 * buf[slot, pl.ds(COMBINE_TM * ROW_PITCH + s, COMBINE_TM, stride=ROW_PITCH), :])
            o_ref[:, cols] = h_ref[:, cols] + gate_ref[0, :, cols] * y
        if final:
            h = o_ref[...]
            o_ref[...] = (h * lax.rsqrt(jnp.mean(h * h, axis=-1, keepdims=True) + EPS)) * fg_ref[...]

    _row_gather_step(pl.program_id(0), pl.num_programs(0), idx_ref, idx_next_ref, ys_ref, buf, sem,
                     2 * COMBINE_TM, consume)


def _combine(h, ys_slabs, pos, info, mods, layer, t_rows, final_g):
    tm = COMBINE_TM
    nt = t_rows // tm
    final = final_g is not None
    idx3 = pos.reshape(2, nt, tm).transpose(1, 0, 2).reshape(nt, 1, 2 * tm)
    in_specs = [pl.BlockSpec((1, 1, 2 * tm), lambda i: (i, 0, 0), memory_space=pltpu.SMEM),
                pl.BlockSpec((1, 1, 2 * tm), lambda i: (jnp.minimum(i + 1, nt - 1), 0, 0),
                             memory_space=pltpu.SMEM),
                pl.BlockSpec(memory_space=pl.ANY),
                pl.BlockSpec((tm, D), lambda i: (i, 0)),
                pl.BlockSpec((tm, LANES), lambda i: (i, 0)),
                _mod_spec(layer, 5, tm)]
    args = [idx3, idx3, ys_slabs, h, info, mods]
    if final:
        in_specs.append(pl.BlockSpec((1, D), lambda i: (0, 0)))
        args.append(final_g.reshape(1, D))
    return pl.pallas_call(
        functools.partial(_combine_body, final=final),
        grid=(nt,),
        in_specs=in_specs,
        out_specs=pl.BlockSpec((tm, D), lambda i: (i, 0)),
        out_shape=jax.ShapeDtypeStruct((t_rows, D), F32),
        scratch_shapes=[pltpu.VMEM((2, 2 * tm * ROW_PITCH, LANES), F32), pltpu.SemaphoreType.DMA((2,))],
        compiler_params=_params("arbitrary"),
        name="moe_combine",
    )(*args)


def _rope_tables():
    rows = S // GRID_W
    row = jnp.repeat(jnp.arange(rows, dtype=F32), GRID_W)
    col = jnp.tile(jnp.arange(GRID_W, dtype=F32), rows)
    n_freq = HD // 4
    inv_freq = ROPE_BASE ** (-jnp.arange(n_freq, dtype=F32) / n_freq)
    ang = jnp.concatenate([row[:, None] * inv_freq, col[:, None] * inv_freq], axis=-1)
    ang = jnp.concatenate([ang, ang], axis=-1)
    sign = jnp.where(jnp.arange(HD) < HD // 2, -1.0, 1.0).astype(F32)
    cos = jnp.concatenate([jnp.cos(ang), jnp.ones((TM, HD), F32)], axis=0)
    sin = jnp.concatenate([jnp.sin(ang) * sign, jnp.zeros((TM, HD), F32)], axis=0)
    scale = HD ** -0.5 * math.log2(math.e)
    return jnp.stack([cos * scale, cos]), jnp.stack([sin * scale, sin])


def kernel(x, c, ctx, c_ctx, ada_w, ada_b, norm_mix_g, norm_ffn_g, attn_w_qkv, attn_lambda, attn_subln_g, attn_w_o, conv_w_in, conv_b_in, conv_w_dw, conv_b_dw, conv_ln_g, conv_ln_b, conv_w_out, conv_b_out, ffn_w_gu, ffn_w_down, moe_router, moe_w_gu, moe_w_down, final_g):
    x_lat, x_ctx = x.reshape(T_LAT, D), ctx.reshape(T_CTX, D)
    h = None
    cond =jnp.concatenate([c, c_ctx[None, :], jnp.zeros((COND_ROWS - B - 1, D), F32)], axis=0)
    mods = _ada_mods(cond, ada_w, ada_b)
    cos_t, sin_t = _rope_tables()
    mix_g = norm_mix_g.reshape(DEPTH, 1, D)
    ffn_g = norm_ffn_g.reshape(DEPTH, 1, D)

    for i in range(DEPTH):
        mix_idx = i // 2
        ffn_idx = i // 2
        ctx_live = i < LAST_CTX_READER
        t_rows = T_ALL if ctx_live else T_LAT
        if i % 2 == 0:
            lambda_init = 0.8 - 0.6 * math.exp(-0.3 * i)
            w_qkv = attn_w_qkv[mix_idx].astype(BF16)
            w_o = attn_w_o[mix_idx].astype(BF16)
            if h is None:
                qkv = _qkv_proj(x_lat, mods, mix_g, i, w_qkv, cos_t, sin_t)
                qkv = _qkv_proj(x_ctx, mods, mix_g, i, w_qkv, cos_t, sin_t, row_off=T_LAT, out=qkv)
                o = _attention(qkv, attn_lambda, attn_subln_g, mix_idx, lambda_init, ctx_live)
                h = _resid_proj(o, w_o, None, x_lat, mods, i, 2, t_rows, "attn_out", n_rows=T_LAT)
                h = _resid_proj(o, w_o, None, x_ctx, mods, i, 2, t_rows, "attn_out", n_rows=T_CTX,
                                row_off=T_LAT, out=h)
            else:
                qkv = _qkv_proj(h, mods, mix_g, i, w_qkv, cos_t, sin_t)
                o = _attention(qkv, attn_lambda, attn_subln_g, mix_idx, lambda_init, ctx_live)
                h = _resid_proj(o, w_o, None, h, mods, i, 2, t_rows, "attn_out")
        else:
            u = _glu_proj(h, mods, mix_g, i, conv_w_in[mix_idx].astype(BF16),
                          (mix_idx, conv_b_in.reshape(-1, 1, 2 * D)), t_rows, "glu", F32, BN_CONV_IN, "conv_in")
            z = _conv_module(u, conv_w_dw, conv_b_dw.reshape(-1, 1, D), conv_ln_g.reshape(-1, 1, D),
                             conv_ln_b.reshape(-1, 1, D), mix_idx, t_rows)
            h = _resid_proj(z, conv_w_out[mix_idx].astype(BF16), (mix_idx, conv_b_out.reshape(-1, 1, D)),
                            h, mods, i, 2, t_rows, "conv_out")
        if i % 2 == 0:
            act = _glu_proj(h, mods, ffn_g, i, ffn_w_gu[ffn_idx].astype(BF16), None, t_rows,
                            "swiglu", BF16, BN_FFN_UP, "ffn_up")
            h = _resid_proj(act, ffn_w_down[ffn_idx].astype(BF16), None, h, mods, i, 5, t_rows, "ffn_down")
        else:
            n_tiles = N_TILES_E if ctx_live else N_TILES_E_LAT
            w_r = jnp.zeros((D, LANES), F32).at[:, :E].set(moe_router[ffn_idx])
            hm_slabs, info = _router(h, mods, ffn_g, i, w_r, t_rows)
            pos, token_of_pos, tile_expert, n_used = _routing_tables(info, t_rows, n_tiles)
            xs = _gather_rows(hm_slabs, token_of_pos)
            ys_slabs = _moe_experts(xs, moe_w_gu, ffn_idx, moe_w_down[ffn_idx].astype(BF16),
                                    tile_expert, n_used)
            h = _combine(h, ys_slabs, pos, info, mods, i, t_rows, final_g if i == DEPTH - 1 else None)
    return h.reshape(B, S, D)
```
